```python
import math
import jax, jax.numpy as jnp
from jax import lax
import numpy as np

D_MODEL = 4096
BATCH = 4
SEQ = 2048
DEPTH = 2
DEC_BATCH = 16
DEC_SEQ = 64
PAST_LEN = 4096

CHUNK = 64
D_MIX = D_MODEL
CONV_CH = D_MIX // 2
CONV_WIDTH = 31
CONV_BUF = CONV_WIDTH - 1
POOL_CH = D_MIX - CONV_CH
POOL_WINDOWS = (2, 4, 8, 16)
POOL_GROUPS = len(POOL_WINDOWS)
POOL_GROUP_CH = POOL_CH // POOL_GROUPS
POOL_BUF = max(POOL_WINDOWS) - 1
HEAD_DIM = 128
N_HEADS = D_MODEL // HEAD_DIM
Q_BLOCK = 128
SB_SCALE = 1.0 / math.sqrt(HEAD_DIM)
N_EXPERTS = 32
TOP_K = 4
D_EXPERT = D_MODEL // 2
SWIGLU_LIMIT = 7.0
SWIGLU_ALPHA = 1.702
EXPERT_BLOCK = 128
RMS_EPS = 1e-6
LN_EPS = 1e-5

kernel_name = 'streaming_convpool_stickbreak_moe_step'


def rms_norm(x, g):
    xf = x.astype(jnp.float32)
    y = xf * lax.rsqrt(jnp.mean(xf * xf, axis=-1, keepdims=True) + RMS_EPS)
    return (y * g.astype(jnp.float32)).astype(x.dtype)


def layer_norm(x, g, b):
    xf = x.astype(jnp.float32)
    mu = jnp.mean(xf, axis=-1, keepdims=True)
    var = jnp.mean(jnp.square(xf - mu), axis=-1, keepdims=True)
    y = (xf - mu) * lax.rsqrt(var + LN_EPS) * g.astype(jnp.float32) + b.astype(jnp.float32)
    return y.astype(x.dtype)


def conv_pool_mixer(h, conv_buf, pool_buf, pos0, w_in_ab, b_in_ab, conv_w, conv_b, ln_g, ln_b,
                    pool_w, pool_b, pool_scale, w_out_ab):
    L = h.shape[1]
    u = h @ w_in_ab + b_in_ab
    a_val = u[..., :CONV_CH]
    a_gate = u[..., CONV_CH:2 * CONV_CH]
    p_in = u[..., 2 * CONV_CH:]
    g = a_val * jax.nn.sigmoid(a_gate)
    g_ext = jnp.concatenate([conv_buf.astype(g.dtype), g], axis=1)
    conv = lax.conv_general_dilated(
        g_ext, conv_w[:, None, :].astype(g.dtype), window_strides=(1,), padding='VALID',
        dimension_numbers=('NWC', 'WIO', 'NWC'), feature_group_count=CONV_CH) + conv_b
    a_out = jax.nn.silu(layer_norm(conv, ln_g, ln_b))
    p_ext = jnp.concatenate([pool_buf.astype(p_in.dtype), p_in], axis=1)
    csum = jnp.cumsum(p_ext.astype(jnp.float32), axis=1)
    csum = jnp.pad(csum, ((0, 0), (1, 0), (0, 0)))
    pos = pos0 + jnp.arange(L, dtype=jnp.int32)
    outs = []
    for gi, win in enumerate(POOL_WINDOWS):
        grp = slice(gi * POOL_GROUP_CH, (gi + 1) * POOL_GROUP_CH)
        wsum = (csum[:, POOL_BUF + 1:POOL_BUF + 1 + L, grp]
                - csum[:, POOL_BUF + 1 - win:POOL_BUF + 1 - win + L, grp])
        count = jnp.minimum(pos + 1, win).astype(jnp.float32)[None, :, None]
        diff = (wsum / count).astype(p_in.dtype) - p_in[..., grp]
        outs.append(diff @ pool_w[gi] + pool_b[gi])
    b_out = jnp.concatenate(outs, axis=-1) * pool_scale
    y = jnp.concatenate([a_out, b_out], axis=-1) @ w_out_ab
    return y, g_ext[:, -CONV_BUF:], p_ext[:, -POOL_BUF:]


def stick_breaking_block(q, k, v, q_pos, k_pos):
    z = jnp.einsum('bqhd,bkhd->bhqk', q, k, preferred_element_type=jnp.float32) * SB_SCALE
    mask = k_pos[None, :] < q_pos[:, None]
    log_beta = jax.nn.log_sigmoid(z)
    log_keep = jnp.where(mask, jax.nn.log_sigmoid(-z), 0.0)
    between = lax.cumsum(log_keep, axis=3, reverse=True) - log_keep
    a = jnp.where(mask, jnp.exp(log_beta + between), 0.0)
    return jnp.einsum('bhqk,bkhd->bqhd', a.astype(v.dtype), v)


def stick_breaking_mixer(h, k_past, v_past, w_qkv, w_o):
    bn, L, _ = h.shape
    qkv = (h @ w_qkv).reshape(bn, L, 3, N_HEADS, HEAD_DIM)
    q, k, v = qkv[:, :, 0], qkv[:, :, 1], qkv[:, :, 2]
    if k_past is None:
        k_all, v_all = k, v
    else:
        k_all = jnp.concatenate([k_past.astype(k.dtype), k], axis=1)
        v_all = jnp.concatenate([v_past.astype(v.dtype), v], axis=1)
    offset = k_all.shape[1] - L
    outs = []
    for r0 in range(0, L, Q_BLOCK):
        r1 = min(r0 + Q_BLOCK, L)
        kend = offset + r1
        outs.append(stick_breaking_block(
            q[:, r0:r1], k_all[:, :kend], v_all[:, :kend],
            jnp.arange(offset + r0, offset + r1, dtype=jnp.int32), jnp.arange(kend, dtype=jnp.int32)))
    o = jnp.concatenate(outs, axis=1).reshape(bn, L, N_HEADS * HEAD_DIM)
    return o @ w_o, k, v


def routed_swiglu_moe(h, layer, w_router, b_router, w_up, b_up, w_down, b_down):
    bn, L, D = h.shape
    T = bn * L
    x2 = h.reshape(T, D)
    logits = (jnp.dot(x2, w_router[layer], preferred_element_type=jnp.float32)
              + b_router[layer].astype(jnp.float32))
    top_val, top_idx = lax.top_k(logits, TOP_K)
    gates = jax.nn.softmax(top_val, axis=-1)
    n_assign = T * TOP_K
    flat_e = top_idx.reshape(-1).astype(jnp.int32)
    order = jnp.argsort(flat_e)
    sorted_e = flat_e[order]
    counts = jnp.bincount(flat_e, length=N_EXPERTS).astype(jnp.int32)
    padded = (counts + EXPERT_BLOCK - 1) // EXPERT_BLOCK * EXPERT_BLOCK
    pad_end = jnp.cumsum(padded)
    pad_start = pad_end - padded
    grp_start = jnp.cumsum(counts) - counts
    slot_sorted = pad_start[sorted_e] + jnp.arange(n_assign, dtype=jnp.int32) - grp_start[sorted_e]
    slot = jnp.zeros((n_assign,), jnp.int32).at[order].set(slot_sorted.astype(jnp.int32))
    n_blocks = -(-n_assign // EXPERT_BLOCK) + N_EXPERTS
    slot_token = jnp.full((n_blocks * EXPERT_BLOCK,), T, jnp.int32).at[slot].set(
        jnp.arange(n_assign, dtype=jnp.int32) // TOP_K)
    block_expert = jnp.minimum(
        jnp.searchsorted(pad_end, jnp.arange(n_blocks, dtype=jnp.int32) * EXPERT_BLOCK, side='right'),
        N_EXPERTS - 1).astype(jnp.int32)
    x_rows = jnp.concatenate([x2, jnp.zeros((1, D), x2.dtype)], axis=0)[slot_token]
    x_rows = x_rows.reshape(n_blocks, EXPERT_BLOCK, D)

    def expert_block(args):
        xb, e = args
        hu = xb @ w_up[layer, e] + b_up[layer, e]
        x_glu = jnp.minimum(hu[:, :D_EXPERT], SWIGLU_LIMIT)
        x_lin = jnp.clip(hu[:, D_EXPERT:], -SWIGLU_LIMIT, SWIGLU_LIMIT)
        act = x_glu * jax.nn.sigmoid(SWIGLU_ALPHA * x_glu) * (x_lin + 1.0)
        return act @ w_down[layer, e] + b_down[layer, e]

    y_rows = lax.map(expert_block, (x_rows, block_expert)).reshape(n_blocks * EXPERT_BLOCK, D)
    y_tok = y_rows[slot].reshape(T, TOP_K, D)
    out = jnp.einsum('tk,tkd->td', gates.astype(y_tok.dtype), y_tok)
    return out.reshape(bn, L, D)


def trunk(x, c, conv_buf, pool_buf, k_past, v_past, pos0, norm_g, w_mod, b_mod,
          w_in_ab, b_in_ab, conv_w, conv_b, ln_g, ln_b, pool_w, pool_b, pool_scale, w_out_ab,
          w_qkv, w_o, w_router, b_router, w_up, b_up, w_down, b_down):
    for layer in range(DEPTH):
        mod = (jax.nn.silu(c) @ w_mod[layer] + b_mod[layer])[:, None, :]
        sh_m, sc_m, gt_m, sh_f, sc_f, gt_f = jnp.split(mod, 6, axis=-1)
        h = rms_norm(x, norm_g[layer, 0]) * (1.0 + sc_m) + sh_m
        if layer % 2 == 0:
            y, conv_new, pool_new = conv_pool_mixer(
                h, conv_buf, pool_buf, pos0, w_in_ab, b_in_ab, conv_w, conv_b, ln_g, ln_b,
                pool_w, pool_b, pool_scale, w_out_ab)
        else:
            y, k_new, v_new = stick_breaking_mixer(h, k_past, v_past, w_qkv, w_o)
        x = x + gt_m * rms_norm(y, norm_g[layer, 1])
        h = rms_norm(x, norm_g[layer, 2]) * (1.0 + sc_f) + sh_f
        y = routed_swiglu_moe(h, layer, w_router, b_router, w_up, b_up, w_down, b_down)
        x = x + gt_f * rms_norm(y, norm_g[layer, 3])
    return x, conv_new, pool_new, k_new, v_new


def setup_inputs(seed: int = 0) -> dict:
    key = jax.random.key(seed)
    ks = jax.random.split(key, 32)

    def nrm(k, shape, scale):
        return jax.random.normal(k, shape, jnp.float32) * scale

    return {
        'x_prompt': nrm(ks[0], (BATCH, SEQ, D_MODEL), 1.0),
        'x_sample': nrm(ks[1], (DEC_BATCH, DEC_SEQ, D_MODEL), 1.0),
        'c_prompt': nrm(ks[2], (BATCH, D_MODEL), 1.0),
        'c_sample': nrm(ks[3], (DEC_BATCH, D_MODEL), 1.0),
        'state_conv': nrm(ks[4], (DEC_BATCH, CONV_BUF, CONV_CH), 0.5),
        'state_pool': nrm(ks[5], (DEC_BATCH, POOL_BUF, POOL_CH), 1.0),
        'cache_k': nrm(ks[6], (DEC_BATCH, PAST_LEN, N_HEADS, HEAD_DIM), 1.0),
        'cache_v': nrm(ks[7], (DEC_BATCH, PAST_LEN, N_HEADS, HEAD_DIM), 1.0),
        'norm_g': 1.0 + nrm(ks[8], (DEPTH, 4, D_MODEL), 0.05),
        'w_mod': nrm(ks[9], (DEPTH, D_MODEL, 6 * D_MODEL), 0.5 * D_MODEL ** -0.5),
        'b_mod': nrm(ks[10], (DEPTH, 6 * D_MODEL), 0.02),
        'w_in_ab': nrm(ks[11], (D_MODEL, 2 * CONV_CH + POOL_CH), D_MODEL ** -0.5),
        'b_in_ab': nrm(ks[12], (2 * CONV_CH + POOL_CH,), 0.02),
        'conv_w': nrm(ks[13], (CONV_WIDTH, CONV_CH), CONV_WIDTH ** -0.5),
        'conv_b': nrm(ks[14], (CONV_CH,), 0.02),
        'ln_g': 1.0 + nrm(ks[15], (CONV_CH,), 0.05),
        'ln_b': nrm(ks[16], (CONV_CH,), 0.02),
        'pool_w': nrm(ks[17], (POOL_GROUPS, POOL_GROUP_CH, POOL_GROUP_CH), POOL_GROUP_CH ** -0.5),
        'pool_b': nrm(ks[18], (POOL_GROUPS, POOL_GROUP_CH), 0.02),
        'pool_scale': 1.0 + nrm(ks[19], (POOL_CH,), 0.1),
        'w_out_ab': nrm(ks[20], (D_MIX, D_MODEL), D_MIX ** -0.5),
        'w_qkv': nrm(ks[21], (D_MODEL, 3 * N_HEADS * HEAD_DIM), D_MODEL ** -0.5),
        'w_o': nrm(ks[22], (N_HEADS * HEAD_DIM, D_MODEL), (N_HEADS * HEAD_DIM) ** -0.5),
        'w_router': nrm(ks[23], (DEPTH, D_MODEL, N_EXPERTS), D_MODEL ** -0.5),
        'b_router': nrm(ks[24], (DEPTH, N_EXPERTS), 0.01),
        'w_up': nrm(ks[25], (DEPTH, N_EXPERTS, D_MODEL, 2 * D_EXPERT), D_MODEL ** -0.5),
        'b_up': nrm(ks[26], (DEPTH, N_EXPERTS, 2 * D_EXPERT), 0.01),
        'w_down': nrm(ks[27], (DEPTH, N_EXPERTS, D_EXPERT, D_MODEL), D_EXPERT ** -0.5),
        'b_down': nrm(ks[28], (DEPTH, N_EXPERTS, D_MODEL), 0.01),
    }


def reference(x_prompt, x_sample, c_prompt, c_sample, state_conv, state_pool, cache_k, cache_v,
              norm_g, w_mod, b_mod, w_in_ab, b_in_ab, conv_w, conv_b, ln_g, ln_b,
              pool_w, pool_b, pool_scale, w_out_ab, w_qkv, w_o,
              w_router, b_router, w_up, b_up, w_down, b_down):
    bp = x_prompt.shape[0]
    zero_conv = jnp.zeros((bp, CONV_BUF, CONV_CH), x_prompt.dtype)
    zero_pool = jnp.zeros((bp, POOL_BUF, POOL_CH), x_prompt.dtype)
    y_prompt, conv_p, pool_p, k_p, v_p = trunk(
        x_prompt, c_prompt, zero_conv, zero_pool, None, None, 0, norm_g, w_mod, b_mod,
        w_in_ab, b_in_ab, conv_w, conv_b, ln_g, ln_b, pool_w, pool_b, pool_scale, w_out_ab,
        w_qkv, w_o, w_router, b_router, w_up, b_up, w_down, b_down)
    y_sample, conv_s, pool_s, k_s, v_s = trunk(
        x_sample, c_sample, state_conv, state_pool, cache_k, cache_v, cache_k.shape[1],
        norm_g, w_mod, b_mod,
        w_in_ab, b_in_ab, conv_w, conv_b, ln_g, ln_b, pool_w, pool_b, pool_scale, w_out_ab,
        w_qkv, w_o, w_router, b_router, w_up, b_up, w_down, b_down)
    return (y_prompt, y_sample, conv_p, conv_s, pool_p, pool_s, k_p, k_s, v_p, v_s)
```

```python
import functools
import math

import numpy as np
import jax
import jax.numpy as jnp
from jax import lax
from jax.experimental import pallas as pl
from jax.experimental.pallas import tpu as pltpu

F32 = jnp.float32
BF16 = jnp.bfloat16

LANES = 128
VMEM_LIMIT = 56 * 1024 * 1024

RMS_EPS = 1e-6
LN_EPS = 1e-5
CONV_WIDTH = 31
CONV_BUF = CONV_WIDTH - 1
POOL_WINDOWS = (2, 4, 8, 16)
POOL_BUF = max(POOL_WINDOWS) - 1
HEAD_DIM = 128
TOP_K = 4
SWIGLU_LIMIT = 7.0
SWIGLU_ALPHA = 1.702
SB_SCALE = 1.0 / math.sqrt(HEAD_DIM)
UNDERFLOW = -104.0

CHUNK = 64
ROW_TILE = 256
MOE_TM = 256
COMBINE_TT = 128
CONV_PAD = 32
POOL_PAD = 16


def _cparams(sem, vmem=VMEM_LIMIT):
    return pltpu.CompilerParams(dimension_semantics=sem, vmem_limit_bytes=vmem)


def _rms(x):
    return x * lax.rsqrt(jnp.mean(x * x, axis=-1, keepdims=True) + RMS_EPS)


def _mod_kernel(c_ref, w_ref, b_ref, o_ref):
    c = c_ref[...]
    s = (c * jax.nn.sigmoid(c)).astype(BF16)
    o_ref[...] = jnp.dot(s, w_ref[...].astype(BF16), preferred_element_type=F32) + b_ref[...]


def _modulation(c_all, w_mod, b_mod, tn=512):
    depth, d, n = w_mod.shape
    ns = c_all.shape[0]
    return pl.pallas_call(
        _mod_kernel,
        out_shape=jax.ShapeDtypeStruct((depth, ns, n), F32),
        grid=(depth, n // tn),
        in_specs=[
            pl.BlockSpec((ns, d), lambda l, j: (0, 0)),
            pl.BlockSpec((None, d, tn), lambda l, j: (l, 0, j)),
            pl.BlockSpec((None, 1, tn), lambda l, j: (l, 0, j)),
        ],
        out_specs=pl.BlockSpec((None, ns, tn), lambda l, j: (l, 0, j)),
        compiler_params=_cparams(("arbitrary", "arbitrary")),
        name="modulation",
    )(c_all, w_mod, b_mod.reshape(depth, 1, n))


def _norm_mod_kernel(x_ref, g_ref, sc_ref, sh_ref, o_ref, *, nch):
    for c in range(nch):
        rows = slice(c * CHUNK, (c + 1) * CHUNK)
        y = _rms(x_ref[rows, :]) * g_ref[...]
        o_ref[rows, :] = (y * (1.0 + sc_ref[c]) + sh_ref[c]).astype(o_ref.dtype)


def _norm_mod(x, g, modc, sc_col, sh_col):
    t, d = x.shape
    nch = ROW_TILE // CHUNK
    mspec = lambda col: pl.BlockSpec((nch, 1, d), lambda i: (i, 0, col))
    return pl.pallas_call(
        functools.partial(_norm_mod_kernel, nch=nch),
        out_shape=jax.ShapeDtypeStruct((t, d), BF16),
        grid=(t // ROW_TILE,),
        in_specs=[
            pl.BlockSpec((ROW_TILE, d), lambda i: (i, 0)),
            pl.BlockSpec((1, d), lambda i: (0, 0)),
            mspec(sc_col), mspec(sh_col),
        ],
        out_specs=pl.BlockSpec((ROW_TILE, d), lambda i: (i, 0)),
        compiler_params=_cparams(("arbitrary",)),
        name="norm_mod",
    )(x, g.reshape(1, d), modc, modc)


def _mm_kernel(a_ref, w_ref, b_ref, o_ref, acc_ref, *, nk):
    k = pl.program_id(2)

    @pl.when(k == 0)
    def _():
        acc_ref[...] = jnp.zeros_like(acc_ref)

    acc_ref[...] += jnp.dot(a_ref[...].astype(BF16), w_ref[...].astype(BF16),
                            preferred_element_type=F32)

    @pl.when(k == nk - 1)
    def _():
        o_ref[...] = (acc_ref[...] + b_ref[...]).astype(o_ref.dtype)


def _matmul(a, w, bias, *, col0=0, ncols=None, out_dtype=F32, tm=1536, tn=1024, tk=512):
    m, kdim = a.shape
    n_total = w.shape[1]
    ncols = n_total if ncols is None else ncols
    tm = min(tm, m)
    tn = min(tn, ncols)
    tk = min(tk, kdim)
    assert m % tm == 0 and ncols % tn == 0 and kdim % tk == 0 and col0 % tn == 0
    joff = col0 // tn
    nk = kdim // tk
    if bias is None:
        bias = jnp.zeros((n_total,), F32)
    return pl.pallas_call(
        functools.partial(_mm_kernel, nk=nk),
        out_shape=jax.ShapeDtypeStruct((m, ncols), out_dtype),
        grid=(m // tm, ncols // tn, nk),
        in_specs=[
            pl.BlockSpec((tm, tk), lambda i, j, k: (i, k)),
            pl.BlockSpec((tk, tn), lambda i, j, k: (k, j + joff)),
            pl.BlockSpec((1, tn), lambda i, j, k: (0, j + joff)),
        ],
        out_specs=pl.BlockSpec((tm, tn), lambda i, j, k: (i, j)),
        scratch_shapes=[pltpu.VMEM((tm, tn), F32)],
        compiler_params=_cparams(("arbitrary", "arbitrary", "arbitrary")),
        name="matmul",
    )(a, w, bias.reshape(1, n_total))


def _conv_kernel(seq_ref, start_ref, av_ref, ag_ref, st_ref, w_ref, cb_ref, lg_ref, lb_ref,
                 o_ref, sto_ref, gbuf, cbuf, *, cblk):
    i = pl.program_id(0)
    ch = av_ref.shape[1]

    @pl.when(start_ref[i] == 1)
    def _():
        gbuf[0:CONV_PAD, :] = st_ref[...]

    @pl.when(start_ref[i] == 0)
    def _():
        gbuf[0:CONV_PAD, :] = gbuf[CHUNK:CHUNK + CONV_PAD, :]

    gbuf[CONV_PAD:CONV_PAD + CHUNK, :] = av_ref[...] * jax.nn.sigmoid(ag_ref[...])

    off = CONV_PAD - CONV_BUF
    for c0 in range(0, ch, cblk):
        cols = slice(c0, c0 + cblk)
        acc = jnp.zeros((CHUNK, cblk), F32)
        for j in range(CONV_WIDTH):
            acc = acc + w_ref[j:j + 1, cols] * gbuf[off + j:off + j + CHUNK, cols]
        cbuf[:, cols] = acc + cb_ref[:, cols]

    conv = cbuf[...]
    mu = jnp.mean(conv, axis=-1, keepdims=True)
    cen = conv - mu
    var = jnp.mean(cen * cen, axis=-1, keepdims=True)
    y = cen * lax.rsqrt(var + LN_EPS) * lg_ref[...] + lb_ref[...]
    o_ref[...] = (y * jax.nn.sigmoid(y)).astype(o_ref.dtype)
    sto_ref[...] = gbuf[CHUNK:CHUNK + CONV_PAD, :]


def _conv_branch(u, state_pad, conv_w, conv_b, ln_g, ln_b, seq_of_chunk, chunk_start):
    t = u.shape[0]
    ch = conv_w.shape[1]
    nseq = state_pad.shape[0]
    nchunks = t // CHUNK
    grid_spec = pltpu.PrefetchScalarGridSpec(
        num_scalar_prefetch=2,
        grid=(nchunks,),
        in_specs=[
            pl.BlockSpec((CHUNK, ch), lambda i, s, f: (i, 0)),
            pl.BlockSpec((CHUNK, ch), lambda i, s, f: (i, 1)),
            pl.BlockSpec((None, CONV_PAD, ch), lambda i, s, f: (s[i], 0, 0)),
            pl.BlockSpec((CONV_WIDTH, ch), lambda i, s, f: (0, 0)),
            pl.BlockSpec((1, ch), lambda i, s, f: (0, 0)),
            pl.BlockSpec((1, ch), lambda i, s, f: (0, 0)),
            pl.BlockSpec((1, ch), lambda i, s, f: (0, 0)),
        ],
        out_specs=[
            pl.BlockSpec((CHUNK, ch), lambda i, s, f: (i, 0)),
            pl.BlockSpec((None, CONV_PAD, ch), lambda i, s, f: (s[i], 0, 0)),
        ],
        scratch_shapes=[pltpu.VMEM((CONV_PAD + CHUNK, ch), F32), pltpu.VMEM((CHUNK, ch), F32)],
    )
    return pl.pallas_call(
        functools.partial(_conv_kernel, cblk=min(256, ch)),
        out_shape=[jax.ShapeDtypeStruct((t, ch), BF16),
                   jax.ShapeDtypeStruct((nseq, CONV_PAD, ch), F32)],
        grid_spec=grid_spec,
        compiler_params=_cparams(("arbitrary",)),
        name="conv_branch",
    )(seq_of_chunk, chunk_start, u, u, state_pad, conv_w, conv_b.reshape(1, ch),
      ln_g.reshape(1, ch), ln_b.reshape(1, ch))


def _pool_kernel(seq_ref, start_ref, pos_ref, p_ref, st_ref, w_ref, b_ref, sc_ref,
                 o_ref, sto_ref, pbuf):
    i = pl.program_id(0)
    gch = w_ref.shape[1]

    @pl.when(start_ref[i] == 1)
    def _():
        pbuf[0:POOL_PAD, :] = st_ref[...]

    @pl.when(start_ref[i] == 0)
    def _():
        pbuf[0:POOL_PAD, :] = pbuf[CHUNK:CHUNK + POOL_PAD, :]

    pbuf[POOL_PAD:POOL_PAD + CHUNK, :] = p_ref[...]

    pos = pos_ref[i] + lax.broadcasted_iota(jnp.int32, (CHUNK, 1), 0)
    for gi, win in enumerate(POOL_WINDOWS):
        cols = slice(gi * gch, (gi + 1) * gch)
        cur = pbuf[POOL_PAD:POOL_PAD + CHUNK, cols]
        wsum = cur
        for j in range(1, win):
            wsum = wsum + pbuf[POOL_PAD - j:POOL_PAD - j + CHUNK, cols]
        count = jnp.minimum(pos + 1, win).astype(F32)
        diff = wsum / count - cur
        y = jnp.dot(diff.astype(BF16), w_ref[gi], preferred_element_type=F32) + b_ref[gi]
        o_ref[:, cols] = (y * sc_ref[:, cols]).astype(o_ref.dtype)
    sto_ref[...] = pbuf[CHUNK:CHUNK + POOL_PAD, :]


def _pool_branch(u, col_block, state_pad, pool_w, pool_b, pool_scale,
                 seq_of_chunk, chunk_start, chunk_pos):
    t = u.shape[0]
    ng, gch, _ = pool_w.shape
    ch = ng * gch
    nseq = state_pad.shape[0]
    grid_spec = pltpu.PrefetchScalarGridSpec(
        num_scalar_prefetch=3,
        grid=(t // CHUNK,),
        in_specs=[
            pl.BlockSpec((CHUNK, ch), lambda i, s, f, p: (i, col_block)),
            pl.BlockSpec((None, POOL_PAD, ch), lambda i, s, f, p: (s[i], 0, 0)),
            pl.BlockSpec((ng, gch, gch), lambda i, s, f, p: (0, 0, 0)),
            pl.BlockSpec((ng, 1, gch), lambda i, s, f, p: (0, 0, 0)),
            pl.BlockSpec((1, ch), lambda i, s, f, p: (0, 0)),
        ],
        out_specs=[
            pl.BlockSpec((CHUNK, ch), lambda i, s, f, p: (i, 0)),
            pl.BlockSpec((None, POOL_PAD, ch), lambda i, s, f, p: (s[i], 0, 0)),
        ],
        scratch_shapes=[pltpu.VMEM((POOL_PAD + CHUNK, ch), F32)],
    )
    return pl.pallas_call(
        _pool_kernel,
        out_shape=[jax.ShapeDtypeStruct((t, ch), BF16),
                   jax.ShapeDtypeStruct((nseq, POOL_PAD, ch), F32)],
        grid_spec=grid_spec,
        compiler_params=_cparams(("arbitrary",)),
        name="pool_branch",
    )(seq_of_chunk, chunk_start, chunk_pos, u, state_pad, pool_w.astype(BF16),
      pool_b.reshape(ng, 1, gch), pool_scale.reshape(1, ch))


def _resid_router_kernel(x_ref, y_ref, g1_ref, g2_ref, gt_ref, sc_ref, sh_ref, wr_ref, br_ref,
                         x1_ref, h_ref, idx_ref, gate_ref, *, nch):
    for c in range(nch):
        rows = slice(c * CHUNK, (c + 1) * CHUNK)
        x1 = x_ref[rows, :] + gt_ref[c] * (_rms(y_ref[rows, :]) * g1_ref[...])
        x1_ref[rows, :] = x1
        h_ref[rows, :] = (_rms(x1) * g2_ref[...]) * (1.0 + sc_ref[c]) + sh_ref[c]

    logits = jnp.dot(h_ref[...].astype(BF16), wr_ref[...], preferred_element_type=F32) + br_ref[...]
    lane = lax.broadcasted_iota(jnp.int32, logits.shape, 1).astype(F32)
    vals, idxs = [], []
    for _ in range(TOP_K):
        m = jnp.max(logits, axis=-1, keepdims=True)
        sel = jnp.min(jnp.where(logits == m, lane, float(LANES)), axis=-1, keepdims=True)
        vals.append(m)
        idxs.append(sel)
        logits = jnp.where(lane == sel, -jnp.inf, logits)
    exps = [jnp.exp(v - vals[0]) for v in vals]
    den = exps[0] + exps[1] + exps[2] + exps[3]
    gate_out = jnp.zeros(logits.shape, F32)
    idx_out = jnp.zeros(logits.shape, F32)
    for k in range(TOP_K):
        gate_out = jnp.where(lane == float(k), exps[k] / den, gate_out)
        idx_out = jnp.where(lane == float(k), idxs[k], idx_out)
    gate_ref[...] = gate_out
    idx_ref[...] = idx_out.astype(jnp.int32)


def _resid_router(x, y, g1, g2, modc, gt_col, sc_col, sh_col, w_router, b_router):
    t, d = x.shape
    ne = w_router.shape[1]
    nch = ROW_TILE // CHUNK
    wr = jnp.zeros((d, LANES), BF16).at[:, :ne].set(w_router.astype(BF16))
    br = jnp.full((1, LANES), -1e30, F32).at[0, :ne].set(b_router.astype(F32))
    mspec = lambda col: pl.BlockSpec((nch, 1, d), lambda i: (i, 0, col))
    row = pl.BlockSpec((ROW_TILE, d), lambda i: (i, 0))
    vec = pl.BlockSpec((1, d), lambda i: (0, 0))
    small = pl.BlockSpec((ROW_TILE, LANES), lambda i: (i, 0))
    return pl.pallas_call(
        functools.partial(_resid_router_kernel, nch=nch),
        out_shape=[jax.ShapeDtypeStruct((t, d), F32), jax.ShapeDtypeStruct((t, d), F32),
                   jax.ShapeDtypeStruct((t, LANES), jnp.int32),
                   jax.ShapeDtypeStruct((t, LANES), F32)],
        grid=(t // ROW_TILE,),
        in_specs=[row, row, vec, vec, mspec(gt_col), mspec(sc_col), mspec(sh_col),
                  pl.BlockSpec((d, LANES), lambda i: (0, 0)),
                  pl.BlockSpec((1, LANES), lambda i: (0, 0))],
        out_specs=[row, row, small, small],
        compiler_params=_cparams(("arbitrary",)),
        name="resid_router",
    )(x, y, g1.reshape(1, d), g2.reshape(1, d), modc, modc, modc, wr, br)


def _moe_gather_kernel(tok_ref, h3_hbm, o_ref, buf, sem, *, tm, nt, nsub):
    i = pl.program_id(0)

    def issue(tile, slot):
        def body(r, carry):
            tok = tok_ref[tile * tm + r]
            pltpu.make_async_copy(h3_hbm.at[tok], buf.at[slot * tm + r], sem.at[slot]).start()
            return carry
        lax.fori_loop(0, tm, body, 0)

    @pl.when(i == 0)
    def _():
        issue(0, 0)

    @pl.when(i + 1 < nt)
    def _():
        issue(i + 1, (i + 1) % 2)

    slot = i % 2
    base = pl.multiple_of(slot * tm, tm)
    pltpu.make_async_copy(h3_hbm.at[pl.ds(0, tm)], buf.at[pl.ds(base, tm)], sem.at[slot]).wait()
    for kk in range(nsub):
        o_ref[:, kk * LANES:(kk + 1) * LANES] = buf[pl.ds(base, tm), kk, :].astype(o_ref.dtype)


def _moe_gather(slot_token, h3, n_tiles):
    t, nsub, _ = h3.shape
    tm = MOE_TM
    grid_spec = pltpu.PrefetchScalarGridSpec(
        num_scalar_prefetch=1,
        grid=(n_tiles,),
        in_specs=[pl.BlockSpec(memory_space=pl.ANY)],
        out_specs=pl.BlockSpec((tm, nsub * LANES), lambda i, tok: (i, 0)),
        scratch_shapes=[pltpu.VMEM((2 * tm, nsub, LANES), F32), pltpu.SemaphoreType.DMA((2,))],
    )
    return pl.pallas_call(
        functools.partial(_moe_gather_kernel, tm=tm, nt=n_tiles, nsub=nsub),
        out_shape=jax.ShapeDtypeStruct((n_tiles * tm, nsub * LANES), BF16),
        grid_spec=grid_spec,
        compiler_params=_cparams(("arbitrary",)),
        name="moe_gather",
    )(slot_token, h3)


def _moe_up_kernel(te_ref, first_ref, nv_ref, x_ref, wg_ref, wl_ref, bg_ref, bl_ref, o_ref,
                   wg_bf, wl_bf):
    i = pl.program_id(1)

    @pl.when(first_ref[i] == 1)
    def _():
        wg_bf[...] = wg_ref[...].astype(BF16)
        wl_bf[...] = wl_ref[...].astype(BF16)

    @pl.when(i < nv_ref[0])
    def _():
        x = x_ref[...]
        hg = jnp.dot(x, wg_bf[...], preferred_element_type=F32) + bg_ref[...]
        hl = jnp.dot(x, wl_bf[...], preferred_element_type=F32) + bl_ref[...]
        x_glu = jnp.minimum(hg, SWIGLU_LIMIT)
        x_lin = jnp.clip(hl, -SWIGLU_LIMIT, SWIGLU_LIMIT)
        act = x_glu * jax.nn.sigmoid(SWIGLU_ALPHA * x_glu) * (x_lin + 1.0)
        o_ref[...] = act.astype(o_ref.dtype)

    @pl.when(i >= nv_ref[0])
    def _():
        o_ref[...] = jnp.zeros_like(o_ref)


def _moe_up(x_sorted, w_up, b_up, layer, tile_expert, tile_first, n_valid, tn=512):
    n_pad, d = x_sorted.shape
    _, ne, _, f2 = w_up.shape
    f = f2 // 2
    tm = MOE_TM
    n_tiles = n_pad // tm
    tn = min(tn, f)
    nj = f // tn
    grid_spec = pltpu.PrefetchScalarGridSpec(
        num_scalar_prefetch=3,
        grid=(nj, n_tiles),
        in_specs=[
            pl.BlockSpec((tm, d), lambda j, i, te, fi, nv: (i, 0)),
            pl.BlockSpec((None, None, d, tn), lambda j, i, te, fi, nv: (layer, te[i], 0, j)),
            pl.BlockSpec((None, None, d, tn), lambda j, i, te, fi, nv: (layer, te[i], 0, nj + j)),
            pl.BlockSpec((None, None, 1, tn), lambda j, i, te, fi, nv: (layer, te[i], 0, j)),
            pl.BlockSpec((None, None, 1, tn), lambda j, i, te, fi, nv: (layer, te[i], 0, nj + j)),
        ],
        out_specs=pl.BlockSpec((tm, tn), lambda j, i, te, fi, nv: (i, j)),
        scratch_shapes=[pltpu.VMEM((d, tn), BF16), pltpu.VMEM((d, tn), BF16)],
    )
    b4 = b_up.reshape(b_up.shape[0], ne, 1, f2)
    return pl.pallas_call(
        _moe_up_kernel,
        out_shape=jax.ShapeDtypeStruct((n_pad, f), BF16),
        grid_spec=grid_spec,
        compiler_params=_cparams(("arbitrary", "arbitrary")),
        name="moe_up",
    )(tile_expert, tile_first, n_valid, x_sorted, w_up, w_up, b4, b4)


def _moe_down_kernel(te_ref, first_ref, nv_ref, a_ref, w_ref, b_ref, o_ref, w_bf):
    i = pl.program_id(1)

    @pl.when(first_ref[i] == 1)
    def _():
        w_bf[...] = w_ref[...].astype(BF16)

    @pl.when(i < nv_ref[0])
    def _():
        o_ref[...] = jnp.dot(a_ref[...], w_bf[...], preferred_element_type=F32) + b_ref[...]

    @pl.when(i >= nv_ref[0])
    def _():
        o_ref[...] = jnp.zeros_like(o_ref)


def _moe_down(act, w_down, b_down, layer, tile_expert, tile_first, n_valid, tn=1024):
    n_pad, f = act.shape
    _, ne, _, d = w_down.shape
    tm = MOE_TM
    n_tiles = n_pad // tm
    tn = min(tn, d)
    nj = d // tn
    grid_spec = pltpu.PrefetchScalarGridSpec(
        num_scalar_prefetch=3,
        grid=(nj, n_tiles),
        in_specs=[
            pl.BlockSpec((tm, f), lambda j, i, te, fi, nv: (i, 0)),
            pl.BlockSpec((None, None, f, tn), lambda j, i, te, fi, nv: (layer, te[i], 0, j)),
            pl.BlockSpec((None, None, 1, tn), lambda j, i, te, fi, nv: (layer, te[i], 0, j)),
        ],
        out_specs=pl.BlockSpec((tm, tn), lambda j, i, te, fi, nv: (i, j)),
        scratch_shapes=[pltpu.VMEM((f, tn), BF16)],
    )
    return pl.pallas_call(
        _moe_down_kernel,
        out_shape=jax.ShapeDtypeStruct((n_pad, d), F32),
        grid_spec=grid_spec,
        compiler_params=_cparams(("arbitrary", "arbitrary")),
        name="moe_down",
    )(tile_expert, tile_first, n_valid, act, w_down, b_down.reshape(b_down.shape[0], ne, 1, d))


def _moe_combine_kernel(slot_ref, y3_hbm, x_ref, gate_ref, g_ref, gt_ref, o_ref, buf, ybuf, sem,
                        *, tt, nt, nsub):
    i = pl.program_id(0)
    rows_per_slot = TOP_K * tt

    def issue(tile, s):
        def body(r, carry):
            for k in range(TOP_K):
                src = slot_ref[(tile * tt + r) * TOP_K + k]
                pltpu.make_async_copy(y3_hbm.at[src], buf.at[s * rows_per_slot + k * tt + r],
                                      sem.at[s]).start()
            return carry
        lax.fori_loop(0, tt, body, 0)

    @pl.when(i == 0)
    def _():
        issue(0, 0)

    @pl.when(i + 1 < nt)
    def _():
        issue(i + 1, (i + 1) % 2)

    s = i % 2
    base = pl.multiple_of(s * rows_per_slot, rows_per_slot)
    pltpu.make_async_copy(y3_hbm.at[pl.ds(0, rows_per_slot)], buf.at[pl.ds(base, rows_per_slot)],
                          sem.at[s]).wait()
    gates = gate_ref[...]
    for kk in range(nsub):
        acc = None
        for k in range(TOP_K):
            piece = gates[:, k:k + 1] * buf[pl.ds(base + k * tt, tt), kk, :]
            acc = piece if acc is None else acc + piece
        ybuf[:, kk * LANES:(kk + 1) * LANES] = acc
    for c in range(tt // CHUNK):
        rows = slice(c * CHUNK, (c + 1) * CHUNK)
        o_ref[rows, :] = x_ref[rows, :] + gt_ref[c] * (_rms(ybuf[rows, :]) * g_ref[...])


def _moe_combine(slot, y3, x1, gates_pad, g, modc, gt_col):
    t, d = x1.shape
    nsub = d // LANES
    tt = COMBINE_TT
    nt = t // tt
    nch = tt // CHUNK
    grid_spec = pltpu.PrefetchScalarGridSpec(
        num_scalar_prefetch=1,
        grid=(nt,),
        in_specs=[
            pl.BlockSpec(memory_space=pl.ANY),
            pl.BlockSpec((tt, d), lambda i, sl: (i, 0)),
            pl.BlockSpec((tt, LANES), lambda i, sl: (i, 0)),
            pl.BlockSpec((1, d), lambda i, sl: (0, 0)),
            pl.BlockSpec((nch, 1, d), lambda i, sl: (i, 0, gt_col)),
        ],
        out_specs=pl.BlockSpec((tt, d), lambda i, sl: (i, 0)),
        scratch_shapes=[pltpu.VMEM((2 * TOP_K * tt, nsub, LANES), F32), pltpu.VMEM((tt, d), F32),
                        pltpu.SemaphoreType.DMA((2,))],
    )
    return pl.pallas_call(
        functools.partial(_moe_combine_kernel, tt=tt, nt=nt, nsub=nsub),
        out_shape=jax.ShapeDtypeStruct((t, d), F32),
        grid_spec=grid_spec,
        compiler_params=_cparams(("arbitrary",)),
        name="moe_combine",
    )(slot, y3, x1, gates_pad, g.reshape(1, d), modc)


def _moe_layer(x1, h, top_idx, gates_pad, layer, w_up, b_up, w_down, b_down, g_post, modc, gt_col):
    t, d = h.shape
    ne = w_up.shape[1]
    tm = MOE_TM
    n_assign = t * TOP_K
    n_tiles = n_assign // tm + ne

    flat_e = top_idx.reshape(-1)
    onehot = (flat_e[:, None] == jnp.arange(ne, dtype=jnp.int32)[None, :]).astype(jnp.int32)
    counts = jnp.sum(onehot, axis=0)
    rank = jnp.take_along_axis(jnp.cumsum(onehot, axis=0) - onehot, flat_e[:, None], axis=1)[:, 0]
    padded = (counts + tm - 1) // tm * tm
    pad_end = jnp.cumsum(padded)
    pad_start = pad_end - padded
    slot = (pad_start[flat_e] + rank).astype(jnp.int32)
    slot_token = jnp.zeros((n_tiles * tm,), jnp.int32).at[slot].set(
        jnp.arange(n_assign, dtype=jnp.int32) // TOP_K)
    n_valid = (pad_end[-1] // tm).astype(jnp.int32)
    tile_ids = jnp.arange(n_tiles, dtype=jnp.int32)
    tile_expert = jnp.searchsorted(pad_end, jnp.minimum(tile_ids, n_valid - 1) * tm,
                                   side='right').astype(jnp.int32)
    tile_expert = jnp.minimum(tile_expert, ne - 1)
    tile_first = jnp.concatenate([jnp.ones((1,), jnp.int32),
                                  (tile_expert[1:] != tile_expert[:-1]).astype(jnp.int32)])

    h3 = h.reshape(t, d // LANES, LANES)
    x_sorted = _moe_gather(slot_token, h3, n_tiles)
    nv = n_valid.reshape(1)
    act = _moe_up(x_sorted, w_up, b_up, layer, tile_expert, tile_first, nv)
    y_sorted = _moe_down(act, w_down, b_down, layer, tile_expert, tile_first, nv)
    y3 = y_sorted.reshape(n_tiles * tm, d // LANES, LANES)
    return _moe_combine(slot, y3, x1, gates_pad, g_post, modc, gt_col)


def _sb_tile(q, kt, vt, carry, mask, tri):
    z = lax.dot_general(q, kt, (((1,), (1,)), ((), ())), preferred_element_type=F32) * SB_SCALE
    sp = jnp.log(1.0 + jnp.exp(-jnp.abs(z)))
    log_beta = jnp.minimum(z, 0.0) - sp
    log_keep = jnp.minimum(-z, 0.0) - sp
    if mask is not None:
        log_keep = jnp.where(mask, log_keep, 0.0)
    hi = log_keep.astype(BF16)
    lo = (log_keep - hi.astype(F32)).astype(BF16)
    suffix = (jnp.dot(hi, tri, preferred_element_type=F32)
              + jnp.dot(lo, tri, preferred_element_type=F32))
    a = jnp.exp(log_beta + (suffix - log_keep) + carry)
    if mask is not None:
        a = jnp.where(mask, a, 0.0)
    out = jnp.dot(a.astype(BF16), vt, preferred_element_type=F32)
    return out, carry + suffix[:, 0:1]


def _tri(tk):
    r = lax.broadcasted_iota(jnp.int32, (tk, tk), 0)
    c = lax.broadcasted_iota(jnp.int32, (tk, tk), 1)
    return jnp.where(r >= c, 1.0, 0.0).astype(BF16)


def _strict_lower(tq, tk):
    r = lax.broadcasted_iota(jnp.int32, (tq, tk), 0)
    c = lax.broadcasted_iota(jnp.int32, (tq, tk), 1)
    return c < r


def _attn_prompt_kernel(q_ref, k_ref, v_ref, o_ref, kb_ref, vb_ref, *, seq, tq):
    kb_ref[...] = k_ref[...].astype(BF16)
    vb_ref[...] = v_ref[...].astype(BF16)
    tri = _tri(tq)
    mask = _strict_lower(tq, tq)

    def q_body(qi, carry_unused):
        r0 = pl.multiple_of(qi * tq, tq)
        q = q_ref[pl.ds(r0, tq), :].astype(BF16)
        acc, carry = _sb_tile(q, kb_ref[pl.ds(r0, tq), :], vb_ref[pl.ds(r0, tq), :],
                              jnp.zeros((tq, 1), F32), mask, tri)

        def k_body(step, state):
            acc, carry = state
            c0 = pl.multiple_of((qi - 1 - step) * tq, tq)
            out, carry = _sb_tile(q, kb_ref[pl.ds(c0, tq), :], vb_ref[pl.ds(c0, tq), :],
                                  carry, None, tri)
            return acc + out, carry

        acc, carry = lax.fori_loop(0, qi, k_body, (acc, carry))
        o_ref[pl.ds(r0, tq), :] = acc.astype(o_ref.dtype)
        return carry_unused

    lax.fori_loop(0, seq // tq, q_body, 0)


def _attn_prompt(q, k, v, nb, seq, tq=128):
    d = q.shape[1]
    nh = d // HEAD_DIM
    blk = pl.BlockSpec((seq, HEAD_DIM), lambda b, h: (b, h))
    return pl.pallas_call(
        functools.partial(_attn_prompt_kernel, seq=seq, tq=tq),
        out_shape=jax.ShapeDtypeStruct((nb * seq, d), BF16),
        grid=(nb, nh),
        in_specs=[blk, blk, blk],
        out_specs=blk,
        scratch_shapes=[pltpu.VMEM((seq, HEAD_DIM), BF16), pltpu.VMEM((seq, HEAD_DIM), BF16)],
        compiler_params=_cparams(("arbitrary", "arbitrary")),
        name="attn_prompt",
    )(q, k, v)


def _attn_sample_kernel(q_ref, kn_ref, vn_ref, kc_ref, vc_ref, o_ref, acc_ref, carry_ref,
                        *, nh, lq, tkc, nsteps):
    t = pl.program_id(1)

    @pl.when(t == 0)
    def _():
        tri = _tri(lq)
        mask = _strict_lower(lq, lq)
        for h in range(nh):
            cols = slice(h * HEAD_DIM, (h + 1) * HEAD_DIM)
            out, carry = _sb_tile(q_ref[:, cols].astype(BF16), kn_ref[:, cols].astype(BF16),
                                  vn_ref[:, cols].astype(BF16), jnp.zeros((lq, 1), F32), mask, tri)
            acc_ref[:, cols] = out
            carry_ref[:, cols] = jnp.broadcast_to(carry, (lq, HEAD_DIM))

    @pl.when(t > 0)
    def _():
        tri = _tri(tkc)
        for h in range(nh):
            cols = slice(h * HEAD_DIM, (h + 1) * HEAD_DIM)
            out, carry = _sb_tile(q_ref[:, cols].astype(BF16), kc_ref[:, h, :].astype(BF16),
                                  vc_ref[:, h, :].astype(BF16), carry_ref[:, cols][:, 0:1],
                                  None, tri)
            acc_ref[:, cols] += out
            carry_ref[:, cols] = jnp.broadcast_to(carry, (lq, HEAD_DIM))

    @pl.when(t == nsteps - 1)
    def _():
        o_ref[...] = acc_ref[...].astype(o_ref.dtype)


def _attn_sample(q, k, v, cache_k, cache_v, row_blk0, lq, tkc=256):
    nb, past, nh, hd = cache_k.shape
    d = nh * hd
    tkc = min(tkc, past)
    nkt = past // tkc
    nsteps = nkt + 1
    new = pl.BlockSpec((lq, d), lambda b, t: (row_blk0 + b, 0))
    cache = pl.BlockSpec((None, tkc, nh, hd), lambda b, t: (b, nkt - jnp.maximum(t, 1), 0, 0))
    return pl.pallas_call(
        functools.partial(_attn_sample_kernel, nh=nh, lq=lq, tkc=tkc, nsteps=nsteps),
        out_shape=jax.ShapeDtypeStruct((nb * lq, d), BF16),
        grid=(nb, nsteps),
        in_specs=[new, new, new, cache, cache],
        out_specs=pl.BlockSpec((lq, d), lambda b, t: (b, 0)),
        scratch_shapes=[pltpu.VMEM((lq, d), F32), pltpu.VMEM((lq, d), F32)],
        compiler_params=_cparams(("arbitrary", "arbitrary")),
        name="attn_sample",
    )(q, k, v, cache_k, cache_v)


def kernel(x_prompt, x_sample, c_prompt, c_sample, state_conv, state_pool, cache_k, cache_v, norm_g, w_mod, b_mod, w_in_ab, b_in_ab, conv_w, conv_b, ln_g, ln_b, pool_w, pool_b, pool_scale, w_out_ab, w_qkv, w_o, w_router, b_router, w_up, b_up, w_down, b_down):
    nbp, lp, d = x_prompt.shape
    nbs, ls, _ = x_sample.shape
    past = cache_k.shape[1]
    conv_ch = conv_w.shape[1]
    pool_ch = pool_scale.shape[0]
    nh = d // HEAD_DIM
    tp, ts = nbp * lp, nbs * ls
    t = tp + ts
    assert lp % CHUNK == 0 and ls == CHUNK and t % ROW_TILE == 0 and tp % COMBINE_TT == 0
    assert ls >= CONV_BUF and lp >= CONV_BUF and conv_ch == pool_ch and d % conv_ch == 0

    cpp = lp // CHUNK
    seq_np = np.concatenate([np.repeat(np.arange(nbp), cpp), nbp + np.arange(nbs)]).astype(np.int32)
    start_np = np.concatenate([np.tile(np.arange(cpp) == 0, nbp), np.ones(nbs, bool)]).astype(np.int32)
    pos_np = np.concatenate([np.tile(np.arange(cpp) * CHUNK, nbp), np.full(nbs, past)]).astype(np.int32)
    seq_of_chunk, chunk_start, chunk_pos = jnp.asarray(seq_np), jnp.asarray(start_np), jnp.asarray(pos_np)

    nseq = nbp + nbs
    nseq_pad = -(-nseq // 16) * 16
    c_all = jnp.zeros((nseq_pad, d), F32).at[:nbp].set(c_prompt).at[nbp:nseq].set(c_sample)
    mod = _modulation(c_all, w_mod, b_mod)
    modc_all = mod[:, seq_of_chunk, :][:, :, None, :]

    x = jnp.concatenate([x_prompt.reshape(tp, d), x_sample.reshape(ts, d)], axis=0)

    conv_state = jnp.zeros((nseq, CONV_PAD, conv_ch), F32).at[nbp:, CONV_PAD - CONV_BUF:].set(state_conv)
    pool_state = jnp.zeros((nseq, POOL_PAD, pool_ch), F32).at[nbp:, POOL_PAD - POOL_BUF:].set(state_pool)

    depth = norm_g.shape[0]
    conv_new = pool_new = k_new = v_new = None
    for layer in range(depth):
        modc = modc_all[layer]
        h = _norm_mod(x, norm_g[layer, 0], modc, sc_col=1, sh_col=0)
        if layer % 2 == 0:
            u = _matmul(h, w_in_ab, b_in_ab)
            a_out, conv_new = _conv_branch(u, conv_state, conv_w, conv_b, ln_g, ln_b,
                                           seq_of_chunk, chunk_start)
            b_out, pool_new = _pool_branch(u, (2 * conv_ch) // pool_ch, pool_state, pool_w, pool_b,
                                           pool_scale, seq_of_chunk, chunk_start, chunk_pos)
            y = _matmul(jnp.concatenate([a_out, b_out], axis=1), w_out_ab, None)
        else:
            q = _matmul(h, w_qkv, None, col0=0, ncols=d)
            k_new = _matmul(h, w_qkv, None, col0=d, ncols=d)
            v_new = _matmul(h, w_qkv, None, col0=2 * d, ncols=d)
            o_p = _attn_prompt(q, k_new, v_new, nbp, lp)
            o_s = _attn_sample(q, k_new, v_new, cache_k, cache_v, tp // ls, ls)
            y = _matmul(jnp.concatenate([o_p, o_s], axis=0), w_o, None)
        x1, hf, top_idx, gates_pad = _resid_router(
            x, y, norm_g[layer, 1], norm_g[layer, 2], modc, 2, 4, 3, w_router[layer], b_router[layer])
        x = _moe_layer(x1, hf, top_idx[:, :TOP_K], gates_pad, layer, w_up, b_up, w_down, b_down,
                       norm_g[layer, 3], modc, 5)

    y_prompt = x[:tp].reshape(nbp, lp, d)
    y_sample = x[tp:].reshape(nbs, ls, d)
    conv_new = conv_new[:, CONV_PAD - CONV_BUF:]
    pool_new = pool_new[:, POOL_PAD - POOL_BUF:]
    k4 = k_new.reshape(t, nh, HEAD_DIM)
    v4 = v_new.reshape(t, nh, HEAD_DIM)
    return (y_prompt, y_sample,
            conv_new[:nbp], conv_new[nbp:], pool_new[:nbp], pool_new[nbp:],
            k4[:tp].reshape(nbp, lp, nh, HEAD_DIM), k4[tp:].reshape(nbs, ls, nh, HEAD_DIM),
            v4[:tp].reshape(nbp, lp, nh, HEAD_DIM), v4[tp:].reshape(nbs, ls, nh, HEAD_DIM))
```

```python
import functools
import math

import numpy as np
import jax
import jax.numpy as jnp
from jax import lax
from jax.experimental import pallas as pl
from jax.experimental.pallas import tpu as pltpu

F32 = jnp.float32
BF16 = jnp.bfloat16

LANES = 128
VMEM_LIMIT = 56 * 1024 * 1024

RMS_EPS = 1e-6
LN_EPS = 1e-5
CONV_WIDTH = 31
CONV_BUF = CONV_WIDTH - 1
POOL_WINDOWS = (2, 4, 8, 16)
POOL_BUF = max(POOL_WINDOWS) - 1
HEAD_DIM = 128
TOP_K = 4
SWIGLU_LIMIT = 7.0
SWIGLU_ALPHA = 1.702
SB_SCALE = 1.0 / math.sqrt(HEAD_DIM)
UNDERFLOW = -104.0

CHUNK = 64
ROW_TILE = 256
MOE_TM = 256
COMBINE_TT = 128
CONV_PAD = 32
POOL_PAD = 16
SB_BLOCK = 128
SB_WINDOW_BLOCKS = 3
SAMPLE_TKC = 256
BIG_POS = 2 ** 30
SLAB_PAD = 8


def _cparams(sem, vmem=VMEM_LIMIT):
    return pltpu.CompilerParams(dimension_semantics=sem, vmem_limit_bytes=vmem)


def _rms(x):
    return x * lax.rsqrt(jnp.mean(x * x, axis=-1, keepdims=True) + RMS_EPS)


def _mod_kernel(c_ref, w_ref, b_ref, o_ref):
    c = c_ref[...]
    s = (c * jax.nn.sigmoid(c)).astype(BF16)
    o_ref[...] = jnp.dot(s, w_ref[...].astype(BF16), preferred_element_type=F32) + b_ref[...]


def _modulation(c_all, w_mod, b_mod, tn=512):
    depth, d, n = w_mod.shape
    ns = c_all.shape[0]
    return pl.pallas_call(
        _mod_kernel,
        out_shape=jax.ShapeDtypeStruct((depth, ns, n), F32),
        grid=(depth, n // tn),
        in_specs=[
            pl.BlockSpec((ns, d), lambda l, j: (0, 0)),
            pl.BlockSpec((None, d, tn), lambda l, j: (l, 0, j)),
            pl.BlockSpec((None, 1, tn), lambda l, j: (l, 0, j)),
        ],
        out_specs=pl.BlockSpec((None, ns, tn), lambda l, j: (l, 0, j)),
        compiler_params=_cparams(("arbitrary", "arbitrary")),
        name="modulation",
    )(c_all, w_mod, b_mod.reshape(depth, 1, n))


def _norm_mod_kernel(x_ref, g_ref, sc_ref, sh_ref, o_ref, *, nch):
    for c in range(nch):
        rows = slice(c * CHUNK, (c + 1) * CHUNK)
        y = _rms(x_ref[rows, :]) * g_ref[...]
        o_ref[rows, :] = (y * (1.0 + sc_ref[c]) + sh_ref[c]).astype(o_ref.dtype)


def _norm_mod(x, g, modc, sc_col, sh_col):
    t, d = x.shape
    nch = ROW_TILE // CHUNK
    mspec = lambda col: pl.BlockSpec((nch, 1, d), lambda i: (i, 0, col))
    return pl.pallas_call(
        functools.partial(_norm_mod_kernel, nch=nch),
        out_shape=jax.ShapeDtypeStruct((t, d), BF16),
        grid=(t // ROW_TILE,),
        in_specs=[
            pl.BlockSpec((ROW_TILE, d), lambda i: (i, 0)),
            pl.BlockSpec((1, d), lambda i: (0, 0)),
            mspec(sc_col), mspec(sh_col),
        ],
        out_specs=pl.BlockSpec((ROW_TILE, d), lambda i: (i, 0)),
        compiler_params=_cparams(("arbitrary",)),
        name="norm_mod",
    )(x, g.reshape(1, d), modc, modc)


def _mm_kernel(a_ref, w_ref, b_ref, o_ref, w_bf):
    @pl.when(pl.program_id(1) == 0)
    def _():
        w_bf[...] = w_ref[...].astype(BF16)

    o_ref[...] = (jnp.dot(a_ref[...], w_bf[...], preferred_element_type=F32)
                  + b_ref[...]).astype(o_ref.dtype)


def _matmul(a, w, bias, *, col0=0, ncols=None, out_dtype=F32, tm=1024, tn=512):
    m, kdim = a.shape
    n_total = w.shape[1]
    ncols = n_total if ncols is None else ncols
    tm = math.gcd(tm, m)
    tn = math.gcd(tn, ncols)
    assert a.dtype == BF16 and tm % 16 == 0 and tn % LANES == 0 and col0 % tn == 0
    joff = col0 // tn
    if bias is None:
        bias = jnp.zeros((n_total,), F32)
    return pl.pallas_call(
        _mm_kernel,
        out_shape=jax.ShapeDtypeStruct((m, ncols), out_dtype),
        grid=(ncols // tn, m // tm),
        in_specs=[
            pl.BlockSpec((tm, kdim), lambda j, i: (i, 0)),
            pl.BlockSpec((kdim, tn), lambda j, i: (0, j + joff)),
            pl.BlockSpec((1, tn), lambda j, i: (0, j + joff)),
        ],
        out_specs=pl.BlockSpec((tm, tn), lambda j, i: (i, j)),
        scratch_shapes=[pltpu.VMEM((kdim, tn), BF16)],
        compiler_params=_cparams(("arbitrary", "arbitrary")),
        name="matmul",
    )(a, w, bias.reshape(1, n_total))


def _conv_kernel(seq_ref, start_ref, av_ref, ag_ref, st_ref, w_ref, cb_ref, lg_ref, lb_ref,
                 o_ref, sto_ref, gbuf, cbuf, *, cblk):
    i = pl.program_id(0)
    ch = av_ref.shape[1]

    @pl.when(start_ref[i] == 1)
    def _():
        gbuf[0:CONV_PAD, :] = st_ref[...]

    @pl.when(start_ref[i] == 0)
    def _():
        gbuf[0:CONV_PAD, :] = gbuf[CHUNK:CHUNK + CONV_PAD, :]

    gbuf[CONV_PAD:CONV_PAD + CHUNK, :] = av_ref[...] * jax.nn.sigmoid(ag_ref[...])

    off = CONV_PAD - CONV_BUF
    for c0 in range(0, ch, cblk):
        cols = slice(c0, c0 + cblk)
        acc = jnp.zeros((CHUNK, cblk), F32)
        for j in range(CONV_WIDTH):
            acc = acc + w_ref[j:j + 1, cols] * gbuf[off + j:off + j + CHUNK, cols]
        cbuf[:, cols] = acc + cb_ref[:, cols]

    conv = cbuf[...]
    mu = jnp.mean(conv, axis=-1, keepdims=True)
    cen = conv - mu
    var = jnp.mean(cen * cen, axis=-1, keepdims=True)
    y = cen * lax.rsqrt(var + LN_EPS) * lg_ref[...] + lb_ref[...]
    o_ref[...] = (y * jax.nn.sigmoid(y)).astype(o_ref.dtype)
    sto_ref[...] = gbuf[CHUNK:CHUNK + CONV_PAD, :]


def _conv_branch(u, state_pad, conv_w, conv_b, ln_g, ln_b, seq_of_chunk, chunk_start):
    t = u.shape[0]
    ch = conv_w.shape[1]
    nseq = state_pad.shape[0]
    nchunks = t // CHUNK
    grid_spec = pltpu.PrefetchScalarGridSpec(
        num_scalar_prefetch=2,
        grid=(nchunks,),
        in_specs=[
            pl.BlockSpec((CHUNK, ch), lambda i, s, f: (i, 0)),
            pl.BlockSpec((CHUNK, ch), lambda i, s, f: (i, 1)),
            pl.BlockSpec((None, CONV_PAD, ch), lambda i, s, f: (s[i], 0, 0)),
            pl.BlockSpec((CONV_WIDTH, ch), lambda i, s, f: (0, 0)),
            pl.BlockSpec((1, ch), lambda i, s, f: (0, 0)),
            pl.BlockSpec((1, ch), lambda i, s, f: (0, 0)),
            pl.BlockSpec((1, ch), lambda i, s, f: (0, 0)),
        ],
        out_specs=[
            pl.BlockSpec((CHUNK, ch), lambda i, s, f: (i, 0)),
            pl.BlockSpec((None, CONV_PAD, ch), lambda i, s, f: (s[i], 0, 0)),
        ],
        scratch_shapes=[pltpu.VMEM((CONV_PAD + CHUNK, ch), F32), pltpu.VMEM((CHUNK, ch), F32)],
    )
    return pl.pallas_call(
        functools.partial(_conv_kernel, cblk=min(256, ch)),
        out_shape=[jax.ShapeDtypeStruct((t, ch), BF16),
                   jax.ShapeDtypeStruct((nseq, CONV_PAD, ch), F32)],
        grid_spec=grid_spec,
        compiler_params=_cparams(("arbitrary",)),
        name="conv_branch",
    )(seq_of_chunk, chunk_start, u, u, state_pad, conv_w, conv_b.reshape(1, ch),
      ln_g.reshape(1, ch), ln_b.reshape(1, ch))


def _pool_kernel(seq_ref, start_ref, pos_ref, p_ref, st_ref, w_ref, b_ref, sc_ref,
                 o_ref, sto_ref, pbuf):
    i = pl.program_id(0)
    gch = w_ref.shape[1]

    @pl.when(start_ref[i] == 1)
    def _():
        pbuf[0:POOL_PAD, :] = st_ref[...]

    @pl.when(start_ref[i] == 0)
    def _():
        pbuf[0:POOL_PAD, :] = pbuf[CHUNK:CHUNK + POOL_PAD, :]

    pbuf[POOL_PAD:POOL_PAD + CHUNK, :] = p_ref[...]

    pos = pos_ref[i] + lax.broadcasted_iota(jnp.int32, (CHUNK, 1), 0)
    for gi, win in enumerate(POOL_WINDOWS):
        cols = slice(gi * gch, (gi + 1) * gch)
        cur = pbuf[POOL_PAD:POOL_PAD + CHUNK, cols]
        wsum = cur
        for j in range(1, win):
            wsum = wsum + pbuf[POOL_PAD - j:POOL_PAD - j + CHUNK, cols]
        count = jnp.minimum(pos + 1, win).astype(F32)
        diff = wsum / count - cur
        y = jnp.dot(diff.astype(BF16), w_ref[gi], preferred_element_type=F32) + b_ref[gi]
        o_ref[:, cols] = (y * sc_ref[:, cols]).astype(o_ref.dtype)
    sto_ref[...] = pbuf[CHUNK:CHUNK + POOL_PAD, :]


def _pool_branch(u, col_block, state_pad, pool_w, pool_b, pool_scale,
                 seq_of_chunk, chunk_start, chunk_pos):
    t = u.shape[0]
    ng, gch, _ = pool_w.shape
    ch = ng * gch
    nseq = state_pad.shape[0]
    grid_spec = pltpu.PrefetchScalarGridSpec(
        num_scalar_prefetch=3,
        grid=(t // CHUNK,),
        in_specs=[
            pl.BlockSpec((CHUNK, ch), lambda i, s, f, p: (i, col_block)),
            pl.BlockSpec((None, POOL_PAD, ch), lambda i, s, f, p: (s[i], 0, 0)),
            pl.BlockSpec((ng, gch, gch), lambda i, s, f, p: (0, 0, 0)),
            pl.BlockSpec((ng, 1, gch), lambda i, s, f, p: (0, 0, 0)),
            pl.BlockSpec((1, ch), lambda i, s, f, p: (0, 0)),
        ],
        out_specs=[
            pl.BlockSpec((CHUNK, ch), lambda i, s, f, p: (i, 0)),
            pl.BlockSpec((None, POOL_PAD, ch), lambda i, s, f, p: (s[i], 0, 0)),
        ],
        scratch_shapes=[pltpu.VMEM((POOL_PAD + CHUNK, ch), F32)],
    )
    return pl.pallas_call(
        _pool_kernel,
        out_shape=[jax.ShapeDtypeStruct((t, ch), BF16),
                   jax.ShapeDtypeStruct((nseq, POOL_PAD, ch), F32)],
        grid_spec=grid_spec,
        compiler_params=_cparams(("arbitrary",)),
        name="pool_branch",
    )(seq_of_chunk, chunk_start, chunk_pos, u, state_pad, pool_w.astype(BF16),
      pool_b.reshape(ng, 1, gch), pool_scale.reshape(1, ch))


def _resid_router_kernel(x_ref, y_ref, g1_ref, g2_ref, gt_ref, sc_ref, sh_ref, wr_ref, br_ref,
                         x1_ref, h_ref, idx_ref, gate_ref, cnt_ref, *, nch):
    for c in range(nch):
        rows = slice(c * CHUNK, (c + 1) * CHUNK)
        x1 = x_ref[rows, :] + gt_ref[c] * (_rms(y_ref[rows, :]) * g1_ref[...])
        x1_ref[rows, :] = x1
        h_ref[rows, :] = (_rms(x1) * g2_ref[...]) * (1.0 + sc_ref[c]) + sh_ref[c]

    logits = jnp.dot(h_ref[...].astype(BF16), wr_ref[...], preferred_element_type=F32) + br_ref[...]
    lane = lax.broadcasted_iota(jnp.int32, logits.shape, 1).astype(F32)
    vals, idxs = [], []
    for _ in range(TOP_K):
        m = jnp.max(logits, axis=-1, keepdims=True)
        sel = jnp.min(jnp.where(logits == m, lane, float(LANES)), axis=-1, keepdims=True)
        vals.append(m)
        idxs.append(sel)
        logits = jnp.where(lane == sel, -jnp.inf, logits)
    exps = [jnp.exp(v - vals[0]) for v in vals]
    den = exps[0] + exps[1] + exps[2] + exps[3]

    @pl.when(pl.program_id(0) == 0)
    def _():
        cnt_ref[...] = jnp.zeros_like(cnt_ref)

    tr = logits.shape[0]
    onehot = jnp.zeros(logits.shape, F32)
    for k in range(TOP_K):
        onehot = onehot + jnp.where(lane == idxs[k], 1.0, 0.0)
    rr = lax.broadcasted_iota(jnp.int32, (tr, tr), 0)
    cc = lax.broadcasted_iota(jnp.int32, (tr, tr), 1)
    lower = jnp.where(cc < rr, 1.0, 0.0).astype(BF16)
    before = jnp.dot(lower, onehot.astype(BF16), preferred_element_type=F32) + cnt_ref[...]
    cnt_ref[...] += jnp.sum(onehot, axis=0, keepdims=True)

    gate_out = jnp.zeros(logits.shape, F32)
    idx_out = jnp.zeros(logits.shape, F32)
    for k in range(TOP_K):
        rank = jnp.sum(jnp.where(lane == idxs[k], before, 0.0), axis=-1, keepdims=True)
        gate_out = jnp.where(lane == float(k), exps[k] / den, gate_out)
        idx_out = jnp.where(lane == float(k), idxs[k], idx_out)
        idx_out = jnp.where(lane == float(TOP_K + k), rank, idx_out)
    gate_ref[...] = gate_out
    idx_ref[...] = idx_out.astype(jnp.int32)


def _resid_router(x, y, g1, g2, modc, gt_col, sc_col, sh_col, w_router, b_router):
    t, d = x.shape
    ne = w_router.shape[1]
    nch = ROW_TILE // CHUNK
    wr = jnp.zeros((d, LANES), BF16).at[:, :ne].set(w_router.astype(BF16))
    br = jnp.full((1, LANES), -1e30, F32).at[0, :ne].set(b_router.astype(F32))
    mspec = lambda col: pl.BlockSpec((nch, 1, d), lambda i: (i, 0, col))
    row = pl.BlockSpec((ROW_TILE, d), lambda i: (i, 0))
    vec = pl.BlockSpec((1, d), lambda i: (0, 0))
    small = pl.BlockSpec((ROW_TILE, LANES), lambda i: (i, 0))
    return pl.pallas_call(
        functools.partial(_resid_router_kernel, nch=nch),
        out_shape=[jax.ShapeDtypeStruct((t, d), F32), jax.ShapeDtypeStruct((t, d), F32),
                   jax.ShapeDtypeStruct((t, LANES), jnp.int32),
                   jax.ShapeDtypeStruct((t, LANES), F32),
                   jax.ShapeDtypeStruct((1, LANES), F32)],
        grid=(t // ROW_TILE,),
        in_specs=[row, row, vec, vec, mspec(gt_col), mspec(sc_col), mspec(sh_col),
                  pl.BlockSpec((d, LANES), lambda i: (0, 0)),
                  pl.BlockSpec((1, LANES), lambda i: (0, 0))],
        out_specs=[row, row, small, small, pl.BlockSpec((1, LANES), lambda i: (0, 0))],
        compiler_params=_cparams(("arbitrary",)),
        name="resid_router",
    )(x, y, g1.reshape(1, d), g2.reshape(1, d), modc, modc, modc, wr, br)


def _moe_gather_kernel(tok_ref, nv_ref, h3_hbm, o_ref, buf, tmp, sem, *, tm, nsub):
    i = pl.program_id(0)
    nv = nv_ref[0]

    def issue(tile, slot):
        def body(r, carry):
            tok = tok_ref[tile * tm + r]
            pltpu.make_async_copy(h3_hbm.at[tok], buf.at[slot * tm + r, pl.ds(0, nsub)],
                                  sem.at[slot]).start()
            return carry
        lax.fori_loop(0, tm, body, 0)

    @pl.when(i == 0)
    def _():
        issue(0, 0)

    @pl.when(i + 1 < nv)
    def _():
        issue(i + 1, (i + 1) % 2)

    @pl.when(i < nv)
    def _():
        slot = i % 2
        base = pl.multiple_of(slot * tm, tm)
        pltpu.make_async_copy(h3_hbm.at[pl.ds(0, tm)], buf.at[pl.ds(base, tm), pl.ds(0, nsub)],
                              sem.at[slot]).wait()
        for kk in range(nsub):
            tmp[:, kk * LANES:(kk + 1) * LANES] = buf[pl.ds(base, tm), kk, :]
        o_ref[...] = tmp[...].astype(o_ref.dtype)

    @pl.when(i >= nv)
    def _():
        o_ref[...] = jnp.zeros_like(o_ref)


def _moe_gather(slot_token, n_valid, h3, n_tiles):
    t, nsub, _ = h3.shape
    tm = MOE_TM
    grid_spec = pltpu.PrefetchScalarGridSpec(
        num_scalar_prefetch=2,
        grid=(n_tiles,),
        in_specs=[pl.BlockSpec(memory_space=pl.ANY)],
        out_specs=pl.BlockSpec((tm, nsub * LANES), lambda i, tok, nv: (i, 0)),
        scratch_shapes=[pltpu.VMEM((2 * tm, nsub + SLAB_PAD, LANES), F32),
                        pltpu.VMEM((tm, nsub * LANES), F32), pltpu.SemaphoreType.DMA((2,))],
    )
    return pl.pallas_call(
        functools.partial(_moe_gather_kernel, tm=tm, nsub=nsub),
        out_shape=jax.ShapeDtypeStruct((n_tiles * tm, nsub * LANES), BF16),
        grid_spec=grid_spec,
        compiler_params=_cparams(("arbitrary",)),
        name="moe_gather",
    )(slot_token, n_valid, h3)


def _moe_up_kernel(te_ref, first_ref, nxt_ref, nv_ref, x_ref, w_hbm, bg_ref, bl_ref, o_ref,
                   stage, w_bf, sem, *, layer, tn, nj):
    j = pl.program_id(0)
    i = pl.program_id(1)

    def copies(e, jj):
        glu = pl.ds(pl.multiple_of(jj * tn, tn), tn)
        lin = pl.ds(pl.multiple_of((nj + jj) * tn, tn), tn)
        return (pltpu.make_async_copy(w_hbm.at[layer, e, :, glu], stage.at[0], sem.at[0]),
                pltpu.make_async_copy(w_hbm.at[layer, e, :, lin], stage.at[1], sem.at[1]))

    def start(e, jj):
        for c in copies(e, jj):
            c.start()

    @pl.when(jnp.logical_and(i == 0, j == 0))
    def _():
        start(te_ref[0], 0)

    @pl.when(first_ref[i] == 1)
    def _():
        for c in copies(te_ref[i], j):
            c.wait()
        w_bf[...] = stage[...].astype(BF16)
        nxt = nxt_ref[i]

        @pl.when(nxt >= 0)
        def _():
            start(nxt, j)

        @pl.when(jnp.logical_and(nxt < 0, j + 1 < nj))
        def _():
            start(te_ref[0], j + 1)

    @pl.when(i < nv_ref[0])
    def _():
        x = x_ref[...]
        hg = jnp.dot(x, w_bf[0], preferred_element_type=F32) + bg_ref[...]
        hl = jnp.dot(x, w_bf[1], preferred_element_type=F32) + bl_ref[...]
        x_glu = jnp.minimum(hg, SWIGLU_LIMIT)
        x_lin = jnp.clip(hl, -SWIGLU_LIMIT, SWIGLU_LIMIT)
        act = x_glu * jax.nn.sigmoid(SWIGLU_ALPHA * x_glu) * (x_lin + 1.0)
        o_ref[...] = act.astype(o_ref.dtype)

    @pl.when(i >= nv_ref[0])
    def _():
        o_ref[...] = jnp.zeros_like(o_ref)


def _moe_up(x_sorted, w_up, b_up, layer, tile_expert, tile_first, tile_next, n_valid, tn=512):
    n_pad, d = x_sorted.shape
    _, ne, _, f2 = w_up.shape
    f = f2 // 2
    tm = MOE_TM
    n_tiles = n_pad // tm
    tn = min(tn, f)
    nj = f // tn
    grid_spec = pltpu.PrefetchScalarGridSpec(
        num_scalar_prefetch=4,
        grid=(nj, n_tiles),
        in_specs=[
            pl.BlockSpec((tm, d), lambda j, i, te, fi, nx, nv: (i, 0)),
            pl.BlockSpec(memory_space=pl.ANY),
            pl.BlockSpec((None, None, 1, tn), lambda j, i, te, fi, nx, nv: (layer, te[i], 0, j)),
            pl.BlockSpec((None, None, 1, tn), lambda j, i, te, fi, nx, nv: (layer, te[i], 0, nj + j)),
        ],
        out_specs=pl.BlockSpec((tm, tn), lambda j, i, te, fi, nx, nv: (i, j)),
        scratch_shapes=[pltpu.VMEM((2, d, tn), F32), pltpu.VMEM((2, d, tn), BF16),
                        pltpu.SemaphoreType.DMA((2,))],
    )
    b4 = b_up.reshape(b_up.shape[0], ne, 1, f2)
    return pl.pallas_call(
        functools.partial(_moe_up_kernel, layer=layer, tn=tn, nj=nj),
        out_shape=jax.ShapeDtypeStruct((n_pad, f), BF16),
        grid_spec=grid_spec,
        compiler_params=_cparams(("arbitrary", "arbitrary")),
        name="moe_up",
    )(tile_expert, tile_first, tile_next, n_valid, x_sorted, w_up, b4, b4)


def _moe_down_kernel(te_ref, first_ref, nxt_ref, nv_ref, a_ref, w_hbm, b_ref, o_ref,
                     stage, w_bf, sem, *, layer, tn, nj):
    j = pl.program_id(0)
    i = pl.program_id(1)

    def copy(e, jj):
        cols = pl.ds(pl.multiple_of(jj * tn, tn), tn)
        return pltpu.make_async_copy(w_hbm.at[layer, e, :, cols], stage, sem.at[0])

    @pl.when(jnp.logical_and(i == 0, j == 0))
    def _():
        copy(te_ref[0], 0).start()

    @pl.when(first_ref[i] == 1)
    def _():
        copy(te_ref[i], j).wait()
        w_bf[...] = stage[...].astype(BF16)
        nxt = nxt_ref[i]

        @pl.when(nxt >= 0)
        def _():
            copy(nxt, j).start()

        @pl.when(jnp.logical_and(nxt < 0, j + 1 < nj))
        def _():
            copy(te_ref[0], j + 1).start()

    @pl.when(i < nv_ref[0])
    def _():
        o_ref[...] = jnp.dot(a_ref[...], w_bf[...], preferred_element_type=F32) + b_ref[...]

    @pl.when(i >= nv_ref[0])
    def _():
        o_ref[...] = jnp.zeros_like(o_ref)


def _moe_down(act, w_down, b_down, layer, tile_expert, tile_first, tile_next, n_valid, tn=1024):
    n_pad, f = act.shape
    _, ne, _, d = w_down.shape
    tm = MOE_TM
    n_tiles = n_pad // tm
    tn = min(tn, d)
    nj = d // tn
    grid_spec = pltpu.PrefetchScalarGridSpec(
        num_scalar_prefetch=4,
        grid=(nj, n_tiles),
        in_specs=[
            pl.BlockSpec((tm, f), lambda j, i, te, fi, nx, nv: (i, 0)),
            pl.BlockSpec(memory_space=pl.ANY),
            pl.BlockSpec((None, None, 1, tn), lambda j, i, te, fi, nx, nv: (layer, te[i], 0, j)),
        ],
        out_specs=pl.BlockSpec((tm, tn), lambda j, i, te, fi, nx, nv: (i, j)),
        scratch_shapes=[pltpu.VMEM((f, tn), F32), pltpu.VMEM((f, tn), BF16),
                        pltpu.SemaphoreType.DMA((1,))],
    )
    return pl.pallas_call(
        functools.partial(_moe_down_kernel, layer=layer, tn=tn, nj=nj),
        out_shape=jax.ShapeDtypeStruct((n_pad, d), F32),
        grid_spec=grid_spec,
        compiler_params=_cparams(("arbitrary", "arbitrary")),
        name="moe_down",
    )(tile_expert, tile_first, tile_next, n_valid, act, w_down,
      b_down.reshape(b_down.shape[0], ne, 1, d))


def _moe_combine_kernel(slot_ref, y3_hbm, x_ref, gate_ref, g_ref, gt_ref, o_ref,
                        buf, sum3, ybuf, sem, *, tt, nt, nsub):
    i = pl.program_id(0)
    rows_per_slot = TOP_K * tt

    def issue(tile, s):
        def body(r, carry):
            for k in range(TOP_K):
                src = slot_ref[(tile * tt + r) * TOP_K + k]
                pltpu.make_async_copy(y3_hbm.at[src],
                                      buf.at[s * rows_per_slot + k * tt + r, pl.ds(0, nsub)],
                                      sem.at[s]).start()
            return carry
        lax.fori_loop(0, tt, body, 0)

    @pl.when(i == 0)
    def _():
        issue(0, 0)

    @pl.when(i + 1 < nt)
    def _():
        issue(i + 1, (i + 1) % 2)

    s = i % 2
    base = pl.multiple_of(s * rows_per_slot, rows_per_slot)
    pltpu.make_async_copy(y3_hbm.at[pl.ds(0, rows_per_slot)],
                          buf.at[pl.ds(base, rows_per_slot), pl.ds(0, nsub)], sem.at[s]).wait()

    def token_body(r, carry):
        acc = gate_ref[r, 0:1, :] * buf[base + r, pl.ds(0, nsub), :]
        for k in range(1, TOP_K):
            acc = acc + gate_ref[r, k:k + 1, :] * buf[base + k * tt + r, pl.ds(0, nsub), :]
        sum3[r, pl.ds(0, nsub), :] = acc
        return carry
    lax.fori_loop(0, tt, token_body, 0)

    for kk in range(nsub):
        ybuf[:, kk * LANES:(kk + 1) * LANES] = sum3[:, kk, :]
    for c in range(tt // CHUNK):
        rows = slice(c * CHUNK, (c + 1) * CHUNK)
        o_ref[rows, :] = x_ref[rows, :] + gt_ref[c] * (_rms(ybuf[rows, :]) * g_ref[...])


def _moe_combine(slot, y3, x1, gates_pad, g, modc, gt_col):
    t, d = x1.shape
    nsub = d // LANES
    tt = COMBINE_TT
    nt = t // tt
    nch = tt // CHUNK
    gates_rep = jnp.broadcast_to(gates_pad[:, :TOP_K, None], (t, TOP_K, LANES))
    grid_spec = pltpu.PrefetchScalarGridSpec(
        num_scalar_prefetch=1,
        grid=(nt,),
        in_specs=[
            pl.BlockSpec(memory_space=pl.ANY),
            pl.BlockSpec((tt, d), lambda i, sl: (i, 0)),
            pl.BlockSpec((tt, TOP_K, LANES), lambda i, sl: (i, 0, 0)),
            pl.BlockSpec((1, d), lambda i, sl: (0, 0)),
            pl.BlockSpec((nch, 1, d), lambda i, sl: (i, 0, gt_col)),
        ],
        out_specs=pl.BlockSpec((tt, d), lambda i, sl: (i, 0)),
        scratch_shapes=[pltpu.VMEM((2 * TOP_K * tt, nsub + SLAB_PAD, LANES), F32),
                        pltpu.VMEM((tt, nsub + SLAB_PAD, LANES), F32),
                        pltpu.VMEM((tt, d), F32), pltpu.SemaphoreType.DMA((2,))],
    )
    return pl.pallas_call(
        functools.partial(_moe_combine_kernel, tt=tt, nt=nt, nsub=nsub),
        out_shape=jax.ShapeDtypeStruct((t, d), F32),
        grid_spec=grid_spec,
        compiler_params=_cparams(("arbitrary",)),
        name="moe_combine",
    )(slot, y3, x1, gates_rep, g.reshape(1, d), modc)


def _moe_layer(x1, h, route, counts, gates_pad, layer, w_up, b_up, w_down, b_down, g_post, modc,
               gt_col):
    t, d = h.shape
    ne = w_up.shape[1]
    tm = MOE_TM
    n_assign = t * TOP_K
    n_tiles = n_assign // tm + ne

    flat_e = route[:, :TOP_K].reshape(-1)
    rank = route[:, TOP_K:2 * TOP_K].reshape(-1)
    padded = (counts + tm - 1) // tm * tm
    pad_end = jnp.cumsum(padded)
    pad_start = pad_end - padded
    slot = (pad_start[flat_e] + rank).astype(jnp.int32)
    slot_token = jnp.zeros((n_tiles * tm,), jnp.int32).at[slot].set(
        jnp.arange(n_assign, dtype=jnp.int32) // TOP_K)
    n_valid = (pad_end[-1] // tm).astype(jnp.int32)
    tile_ids = jnp.arange(n_tiles, dtype=jnp.int32)
    tile_expert = jnp.searchsorted(pad_end, jnp.minimum(tile_ids, n_valid - 1) * tm,
                                   side='right').astype(jnp.int32)
    tile_expert = jnp.minimum(tile_expert, ne - 1)
    tile_first = jnp.concatenate([jnp.ones((1,), jnp.int32),
                                  (tile_expert[1:] != tile_expert[:-1]).astype(jnp.int32)])
    first_pos = jnp.where(tile_first == 1, tile_ids, n_tiles)
    next_pos = jnp.concatenate([lax.cummin(first_pos, reverse=True)[1:],
                                jnp.full((1,), n_tiles, jnp.int32)])
    tile_next = jnp.where(next_pos < n_tiles, tile_expert[jnp.minimum(next_pos, n_tiles - 1)], -1)
    tile_next = tile_next.astype(jnp.int32)

    h3 = h.reshape(t, d // LANES, LANES)
    nv = n_valid.reshape(1)
    x_sorted = _moe_gather(slot_token, nv, h3, n_tiles)
    act = _moe_up(x_sorted, w_up, b_up, layer, tile_expert, tile_first, tile_next, nv)
    y_sorted = _moe_down(act, w_down, b_down, layer, tile_expert, tile_first, tile_next, nv)
    y3 = y_sorted.reshape(n_tiles * tm, d // LANES, LANES)
    return _moe_combine(slot, y3, x1, gates_pad, g_post, modc, gt_col)


def _log_gates(z):
    sp = jnp.log(1.0 + jnp.exp(-jnp.abs(z)))
    return jnp.minimum(z, 0.0) - sp, jnp.minimum(-z, 0.0) - sp


def _split_bf16(x):
    hi = x.astype(BF16)
    return hi, (x - hi.astype(F32)).astype(BF16)


def _sb_window(q, k_ref, v_ref, k0, nblk, carry, qpos, klim, tri):
    tb = SB_BLOCK
    tq = q.shape[0]
    col = lax.broadcasted_iota(jnp.int32, (tq, tb), 1)
    lim = jnp.minimum(qpos, klim)
    log_beta, log_keep, masks = [], [], []
    for b in range(nblk):
        kt = k_ref[pl.ds(pl.multiple_of(k0 + b * tb, tb), tb), :]
        z = lax.dot_general(q, kt, (((1,), (1,)), ((), ())), preferred_element_type=F32) * SB_SCALE
        lb, lk = _log_gates(z)
        mask = (k0 + b * tb + col) < lim
        log_beta.append(lb)
        log_keep.append(jnp.where(mask, lk, 0.0))
        masks.append(mask)
    hi, lo = _split_bf16(jnp.concatenate(log_keep, axis=0))
    suf = jnp.dot(jnp.concatenate([hi, lo], axis=0), tri, preferred_element_type=F32)
    a_blocks = [None] * nblk
    for b in reversed(range(nblk)):
        s = suf[b * tq:(b + 1) * tq] + suf[(nblk + b) * tq:(nblk + b + 1) * tq]
        a = jnp.exp(log_beta[b] + (s - log_keep[b]) + carry)
        a_blocks[b] = jnp.where(masks[b], a, 0.0).astype(BF16)
        carry = carry + s[:, 0:1]
    vw = v_ref[pl.ds(pl.multiple_of(k0, tb), nblk * tb), :]
    out = jnp.dot(jnp.concatenate(a_blocks, axis=1), vw, preferred_element_type=F32)
    return out, carry


def _alive(carry):
    return (jnp.max(carry) > UNDERFLOW).astype(jnp.int32)


def _tri(tk):
    r = lax.broadcasted_iota(jnp.int32, (tk, tk), 0)
    c = lax.broadcasted_iota(jnp.int32, (tk, tk), 1)
    return jnp.where(r >= c, 1.0, 0.0).astype(BF16)


def _strict_lower(tq, tk):
    r = lax.broadcasted_iota(jnp.int32, (tq, tk), 0)
    c = lax.broadcasted_iota(jnp.int32, (tq, tk), 1)
    return c < r


def _attn_prompt_kernel(q_ref, k_ref, v_ref, o_ref, kb_ref, vb_ref, *, seq, nblk):
    tb = SB_BLOCK
    kb_ref[...] = k_ref[...].astype(BF16)
    vb_ref[...] = v_ref[...].astype(BF16)
    tri = _tri(tb)
    row = lax.broadcasted_iota(jnp.int32, (tb, 1), 0)

    def first_window(qi):
        r0 = pl.multiple_of(qi * tb, tb)
        q = q_ref[pl.ds(r0, tb), :].astype(BF16)
        qpos = r0 + row
        ws = jnp.maximum(qi - (nblk - 1), 0)
        acc, carry = _sb_window(q, kb_ref, vb_ref, ws * tb, nblk, jnp.zeros((tb, 1), F32),
                                qpos, BIG_POS, tri)
        return r0, q, qpos, ws, acc, carry

    def finish(r0, q, qpos, ws, acc, carry):
        def cond(st):
            return jnp.logical_and(st[0] > 0, st[1] > 0)

        def body(st):
            ws, _, acc, carry = st
            ws2 = jnp.maximum(ws - nblk, 0)
            out, carry = _sb_window(q, kb_ref, vb_ref, ws2 * tb, nblk, carry, qpos, ws * tb, tri)
            return ws2, _alive(carry), acc + out, carry

        _, _, acc, _ = lax.while_loop(cond, body, (ws, _alive(carry), acc, carry))
        o_ref[pl.ds(r0, tb), :] = acc.astype(o_ref.dtype)

    def pair_body(p, unused):
        t0 = first_window(2 * p)
        t1 = first_window(2 * p + 1)
        finish(*t0)
        finish(*t1)
        return unused

    lax.fori_loop(0, seq // (2 * tb), pair_body, 0)


def _attn_prompt(q, k, v, nb, seq):
    d = q.shape[1]
    nh = d // HEAD_DIM
    assert seq % (2 * SB_BLOCK) == 0
    nblk = min(SB_WINDOW_BLOCKS, seq // SB_BLOCK)
    blk = pl.BlockSpec((seq, HEAD_DIM), lambda b, h: (b, h))
    return pl.pallas_call(
        functools.partial(_attn_prompt_kernel, seq=seq, nblk=nblk),
        out_shape=jax.ShapeDtypeStruct((nb * seq, d), BF16),
        grid=(nb, nh),
        in_specs=[blk, blk, blk],
        out_specs=blk,
        scratch_shapes=[pltpu.VMEM((seq, HEAD_DIM), BF16), pltpu.VMEM((seq, HEAD_DIM), BF16)],
        compiler_params=_cparams(("arbitrary", "arbitrary")),
        name="attn_prompt",
    )(q, k, v)


def _attn_sample_kernel(q_ref, kn_ref, vn_ref, ck_hbm, cv_hbm, o_ref,
                        kbuf, vbuf, lb_s, lk_s, hl_s, suf_s, acc_ref, carry_ref, sem,
                        *, nh, lq, tkc, nkt):
    b = pl.program_id(0)
    nrow = nh * lq

    def copies(t, slot):
        k0 = pl.multiple_of((nkt - 1 - t) * tkc, tkc)
        dst = pl.ds(pl.multiple_of(slot * tkc, tkc), tkc)
        return (pltpu.make_async_copy(ck_hbm.at[b, pl.ds(k0, tkc)], kbuf.at[dst], sem.at[0, slot]),
                pltpu.make_async_copy(cv_hbm.at[b, pl.ds(k0, tkc)], vbuf.at[dst], sem.at[1, slot]))

    def start(t, slot):
        for c in copies(t, slot):
            c.start()

    def wait(t, slot):
        for c in copies(t, slot):
            c.wait()

    def heads_tile(get_k, get_v, tk, mask, first):
        tri = _tri(tk)
        for h in range(nh):
            cols = slice(h * HEAD_DIM, (h + 1) * HEAD_DIM)
            rows = slice(h * lq, (h + 1) * lq)
            z = lax.dot_general(q_ref[:, cols].astype(BF16), get_k(h), (((1,), (1,)), ((), ())),
                                preferred_element_type=F32) * SB_SCALE
            lb, lk = _log_gates(z)
            if mask is not None:
                lk = jnp.where(mask, lk, 0.0)
            hi, lo = _split_bf16(lk)
            lb_s[rows, 0:tk] = lb
            lk_s[rows, 0:tk] = lk
            hl_s[rows, 0:tk] = hi
            hl_s[nrow + h * lq:nrow + (h + 1) * lq, 0:tk] = lo
        suf = jnp.dot(hl_s[:, 0:tk], tri, preferred_element_type=F32)
        suf_s[:, 0:tk] = suf[0:nrow] + suf[nrow:2 * nrow]
        for h in range(nh):
            cols = slice(h * HEAD_DIM, (h + 1) * HEAD_DIM)
            rows = slice(h * lq, (h + 1) * lq)
            s = suf_s[rows, 0:tk]
            carry = jnp.zeros((lq, 1), F32) if first else carry_ref[:, cols][:, 0:1]
            a = jnp.exp(lb_s[rows, 0:tk] + (s - lk_s[rows, 0:tk]) + carry)
            if mask is not None:
                a = jnp.where(mask, a, 0.0)
            out = jnp.dot(a.astype(BF16), get_v(h), preferred_element_type=F32)
            if first:
                acc_ref[:, cols] = out
            else:
                acc_ref[:, cols] += out
            carry_ref[:, cols] = jnp.broadcast_to(carry + s[:, 0:1], (lq, HEAD_DIM))

    start(0, 0)
    heads_tile(lambda h: kn_ref[:, h * HEAD_DIM:(h + 1) * HEAD_DIM].astype(BF16),
               lambda h: vn_ref[:, h * HEAD_DIM:(h + 1) * HEAD_DIM].astype(BF16),
               lq, _strict_lower(lq, lq), True)

    def cond(st):
        return jnp.logical_and(st[0] < nkt, st[1] > 0)

    def body(st):
        t = st[0]
        slot = t % 2
        wait(t, slot)

        @pl.when(t + 1 < nkt)
        def _():
            start(t + 1, 1 - slot)

        rows = pl.ds(pl.multiple_of(slot * tkc, tkc), tkc)
        heads_tile(lambda h: kbuf[rows, h, :].astype(BF16), lambda h: vbuf[rows, h, :].astype(BF16),
                   tkc, None, False)
        return t + 1, _alive(carry_ref[...])

    t_end, _ = lax.while_loop(cond, body, (jnp.int32(0), _alive(carry_ref[...])))

    @pl.when(t_end < nkt)
    def _():
        wait(t_end, t_end % 2)

    o_ref[...] = acc_ref[...].astype(o_ref.dtype)


def _attn_sample(q, k, v, cache_k, cache_v, row_blk0, lq):
    nb, past, nh, hd = cache_k.shape
    d = nh * hd
    tkc = min(SAMPLE_TKC, past)
    assert past % tkc == 0
    nkt = past // tkc
    nrow = nh * lq
    new = pl.BlockSpec((lq, d), lambda b: (row_blk0 + b, 0))
    hbm = pl.BlockSpec(memory_space=pl.ANY)
    return pl.pallas_call(
        functools.partial(_attn_sample_kernel, nh=nh, lq=lq, tkc=tkc, nkt=nkt),
        out_shape=jax.ShapeDtypeStruct((nb * lq, d), BF16),
        grid=(nb,),
        in_specs=[new, new, new, hbm, hbm],
        out_specs=pl.BlockSpec((lq, d), lambda b: (b, 0)),
        scratch_shapes=[
            pltpu.VMEM((2 * tkc, nh, hd), F32), pltpu.VMEM((2 * tkc, nh, hd), F32),
            pltpu.VMEM((nrow, tkc), F32), pltpu.VMEM((nrow, tkc), F32),
            pltpu.VMEM((2 * nrow, tkc), BF16), pltpu.VMEM((nrow, tkc), F32),
            pltpu.VMEM((lq, d), F32), pltpu.VMEM((lq, d), F32),
            pltpu.SemaphoreType.DMA((2, 2)),
        ],
        compiler_params=_cparams(("arbitrary",)),
        name="attn_sample",
    )(q, k, v, cache_k, cache_v)


def kernel(x_prompt, x_sample, c_prompt, c_sample, state_conv, state_pool, cache_k, cache_v, norm_g, w_mod, b_mod, w_in_ab, b_in_ab, conv_w, conv_b, ln_g, ln_b, pool_w, pool_b, pool_scale, w_out_ab, w_qkv, w_o, w_router, b_router, w_up, b_up, w_down, b_down):
    nbp, lp, d = x_prompt.shape
    nbs, ls, _ = x_sample.shape
    past = cache_k.shape[1]
    conv_ch = conv_w.shape[1]
    pool_ch = pool_scale.shape[0]
    nh = d // HEAD_DIM
    tp, ts = nbp * lp, nbs * ls
    t = tp + ts
    assert lp % CHUNK == 0 and ls == CHUNK and t % ROW_TILE == 0 and tp % COMBINE_TT == 0
    assert ls >= CONV_BUF and lp >= CONV_BUF and conv_ch == pool_ch and d % conv_ch == 0

    cpp = lp // CHUNK
    seq_np = np.concatenate([np.repeat(np.arange(nbp), cpp), nbp + np.arange(nbs)]).astype(np.int32)
    start_np = np.concatenate([np.tile(np.arange(cpp) == 0, nbp), np.ones(nbs, bool)]).astype(np.int32)
    pos_np = np.concatenate([np.tile(np.arange(cpp) * CHUNK, nbp), np.full(nbs, past)]).astype(np.int32)
    seq_of_chunk, chunk_start, chunk_pos = jnp.asarray(seq_np), jnp.asarray(start_np), jnp.asarray(pos_np)

    nseq = nbp + nbs
    nseq_pad = -(-nseq // 16) * 16
    c_all = jnp.zeros((nseq_pad, d), F32).at[:nbp].set(c_prompt).at[nbp:nseq].set(c_sample)
    mod = _modulation(c_all, w_mod, b_mod)
    modc_all = mod[:, seq_of_chunk, :][:, :, None, :]

    x = jnp.concatenate([x_prompt.reshape(tp, d), x_sample.reshape(ts, d)], axis=0)

    conv_state = jnp.zeros((nseq, CONV_PAD, conv_ch), F32).at[nbp:, CONV_PAD - CONV_BUF:].set(state_conv)
    pool_state = jnp.zeros((nseq, POOL_PAD, pool_ch), F32).at[nbp:, POOL_PAD - POOL_BUF:].set(state_pool)

    depth = norm_g.shape[0]
    conv_new = pool_new = k_new = v_new = None
    for layer in range(depth):
        modc = modc_all[layer]
        h = _norm_mod(x, norm_g[layer, 0], modc, sc_col=1, sh_col=0)
        if layer % 2 == 0:
            u = _matmul(h, w_in_ab, b_in_ab)
            a_out, conv_new = _conv_branch(u, conv_state, conv_w, conv_b, ln_g, ln_b,
                                           seq_of_chunk, chunk_start)
            b_out, pool_new = _pool_branch(u, (2 * conv_ch) // pool_ch, pool_state, pool_w, pool_b,
                                           pool_scale, seq_of_chunk, chunk_start, chunk_pos)
            y = _matmul(jnp.concatenate([a_out, b_out], axis=1), w_out_ab, None)
        else:
            q = _matmul(h, w_qkv, None, col0=0, ncols=d)
            k_new = _matmul(h, w_qkv, None, col0=d, ncols=d)
            v_new = _matmul(h, w_qkv, None, col0=2 * d, ncols=d)
            o_p = _attn_prompt(q, k_new, v_new, nbp, lp)
            o_s = _attn_sample(q, k_new, v_new, cache_k, cache_v, tp // ls, ls)
            y = _matmul(jnp.concatenate([o_p, o_s], axis=0), w_o, None)
        x1, hf, route, gates_pad, counts = _resid_router(
            x, y, norm_g[layer, 1], norm_g[layer, 2], modc, 2, 4, 3, w_router[layer], b_router[layer])
        counts = counts[0, :w_router.shape[2]].astype(jnp.int32)
        x = _moe_layer(x1, hf, route, counts, gates_pad, layer, w_up, b_up, w_down, b_down,
                       norm_g[layer, 3], modc, 5)

    y_prompt = x[:tp].reshape(nbp, lp, d)
    y_sample = x[tp:].reshape(nbs, ls, d)
    conv_new = conv_new[:, CONV_PAD - CONV_BUF:]
    pool_new = pool_new[:, POOL_PAD - POOL_BUF:]
    k4 = k_new.reshape(t, nh, HEAD_DIM)
    v4 = v_new.reshape(t, nh, HEAD_DIM)
    return (y_prompt, y_sample,
            conv_new[:nbp], conv_new[nbp:], pool_new[:nbp], pool_new[nbp:],
            k4[:tp].reshape(nbp, lp, nh, HEAD_DIM), k4[tp:].reshape(nbs, ls, nh, HEAD_DIM),
            v4[:tp].reshape(nbp, lp, nh, HEAD_DIM), v4[tp:].reshape(nbs, ls, nh, HEAD_DIM))
```

```python
import functools
import math

import numpy as np
import jax
import jax.numpy as jnp
from jax import lax
from jax.experimental import pallas as pl
from jax.experimental.pallas import tpu as pltpu

F32 = jnp.float32
BF16 = jnp.bfloat16

LANES = 128
VMEM_LIMIT = 56 * 1024 * 1024

RMS_EPS = 1e-6
LN_EPS = 1e-5
CONV_WIDTH = 31
CONV_BUF = CONV_WIDTH - 1
POOL_WINDOWS = (2, 4, 8, 16)
POOL_BUF = max(POOL_WINDOWS) - 1
HEAD_DIM = 128
TOP_K = 4
SWIGLU_LIMIT = 7.0
SWIGLU_ALPHA = 1.702
SB_SCALE = 1.0 / math.sqrt(HEAD_DIM)
UNDERFLOW = -104.0

CHUNK = 64
ROW_TILE = 256
MOE_TM = 256
COMBINE_TT = 128
CONV_PAD = 32
POOL_PAD = 16
SB_BLOCK = 128
SB_WINDOW_BLOCKS = 3
SB_TILES_PER_STEP = 4
SAMPLE_TKC = 256
BIG_POS = 2 ** 30
SLAB_PAD = 8


def _cparams(sem, vmem=VMEM_LIMIT):
    return pltpu.CompilerParams(dimension_semantics=sem, vmem_limit_bytes=vmem)


def _rms(x):
    return x * lax.rsqrt(jnp.mean(x * x, axis=-1, keepdims=True) + RMS_EPS)


def _mod_kernel(c_ref, w_ref, b_ref, o_ref):
    c = c_ref[...]
    s = (c * jax.nn.sigmoid(c)).astype(BF16)
    o_ref[...] = jnp.dot(s, w_ref[...].astype(BF16), preferred_element_type=F32) + b_ref[...]


def _modulation(c_all, w_mod, b_mod, tn=512):
    depth, d, n = w_mod.shape
    ns = c_all.shape[0]
    return pl.pallas_call(
        _mod_kernel,
        out_shape=jax.ShapeDtypeStruct((depth, ns, n), F32),
        grid=(depth, n // tn),
        in_specs=[
            pl.BlockSpec((ns, d), lambda l, j: (0, 0)),
            pl.BlockSpec((None, d, tn), lambda l, j: (l, 0, j)),
            pl.BlockSpec((None, 1, tn), lambda l, j: (l, 0, j)),
        ],
        out_specs=pl.BlockSpec((None, ns, tn), lambda l, j: (l, 0, j)),
        compiler_params=_cparams(("arbitrary", "arbitrary")),
        name="modulation",
    )(c_all, w_mod, b_mod.reshape(depth, 1, n))


def _norm_mod_kernel(x_ref, g_ref, sc_ref, sh_ref, o_ref, *, nch):
    for c in range(nch):
        rows = slice(c * CHUNK, (c + 1) * CHUNK)
        y = _rms(x_ref[rows, :]) * g_ref[...]
        o_ref[rows, :] = (y * (1.0 + sc_ref[c]) + sh_ref[c]).astype(o_ref.dtype)


def _norm_mod(x, g, modc, sc_col, sh_col):
    t, d = x.shape
    nch = ROW_TILE // CHUNK
    mspec = lambda col: pl.BlockSpec((nch, 1, d), lambda i: (i, 0, col))
    return pl.pallas_call(
        functools.partial(_norm_mod_kernel, nch=nch),
        out_shape=jax.ShapeDtypeStruct((t, d), BF16),
        grid=(t // ROW_TILE,),
        in_specs=[
            pl.BlockSpec((ROW_TILE, d), lambda i: (i, 0)),
            pl.BlockSpec((1, d), lambda i: (0, 0)),
            mspec(sc_col), mspec(sh_col),
        ],
        out_specs=pl.BlockSpec((ROW_TILE, d), lambda i: (i, 0)),
        compiler_params=_cparams(("arbitrary",)),
        name="norm_mod",
    )(x, g.reshape(1, d), modc, modc)


def _mm_kernel(a_ref, w_ref, b_ref, o_ref, w_bf):
    @pl.when(pl.program_id(1) == 0)
    def _():
        w_bf[...] = w_ref[...].astype(BF16)

    o_ref[...] = (jnp.dot(a_ref[...], w_bf[...], preferred_element_type=F32)
                  + b_ref[...]).astype(o_ref.dtype)


def _matmul(a, w, bias, *, col0=0, ncols=None, row0=0, nrows=None, out_dtype=F32, tm=1024, tn=512):
    kdim = a.shape[1]
    m = a.shape[0] if nrows is None else nrows
    n_total = w.shape[1]
    ncols = n_total if ncols is None else ncols
    tm = math.gcd(tm, m)
    tn = math.gcd(tn, ncols)
    assert a.dtype == BF16 and tm % 16 == 0 and tn % LANES == 0 and col0 % tn == 0 and row0 % tm == 0
    joff = col0 // tn
    ioff = row0 // tm
    if bias is None:
        bias = jnp.zeros((n_total,), F32)
    return pl.pallas_call(
        _mm_kernel,
        out_shape=jax.ShapeDtypeStruct((m, ncols), out_dtype),
        grid=(ncols // tn, m // tm),
        in_specs=[
            pl.BlockSpec((tm, kdim), lambda j, i: (i + ioff, 0)),
            pl.BlockSpec((kdim, tn), lambda j, i: (0, j + joff)),
            pl.BlockSpec((1, tn), lambda j, i: (0, j + joff)),
        ],
        out_specs=pl.BlockSpec((tm, tn), lambda j, i: (i, j)),
        scratch_shapes=[pltpu.VMEM((kdim, tn), BF16)],
        compiler_params=_cparams(("arbitrary", "arbitrary")),
        name="matmul",
    )(a, w, bias.reshape(1, n_total))


def _conv_kernel(seq_ref, start_ref, av_ref, ag_ref, st_ref, w_ref, cb_ref, lg_ref, lb_ref,
                 o_ref, sto_ref, gbuf, cbuf, *, cblk):
    i = pl.program_id(0)
    ch = av_ref.shape[1]

    @pl.when(start_ref[i] == 1)
    def _():
        gbuf[0:CONV_PAD, :] = st_ref[...]

    @pl.when(start_ref[i] == 0)
    def _():
        gbuf[0:CONV_PAD, :] = gbuf[CHUNK:CHUNK + CONV_PAD, :]

    gbuf[CONV_PAD:CONV_PAD + CHUNK, :] = av_ref[...] * jax.nn.sigmoid(ag_ref[...])

    off = CONV_PAD - CONV_BUF
    for c0 in range(0, ch, cblk):
        cols = slice(c0, c0 + cblk)
        acc = jnp.zeros((CHUNK, cblk), F32)
        for j in range(CONV_WIDTH):
            acc = acc + w_ref[j:j + 1, cols] * gbuf[off + j:off + j + CHUNK, cols]
        cbuf[:, cols] = acc + cb_ref[:, cols]

    conv = cbuf[...]
    mu = jnp.mean(conv, axis=-1, keepdims=True)
    cen = conv - mu
    var = jnp.mean(cen * cen, axis=-1, keepdims=True)
    y = cen * lax.rsqrt(var + LN_EPS) * lg_ref[...] + lb_ref[...]
    o_ref[...] = (y * jax.nn.sigmoid(y)).astype(o_ref.dtype)
    sto_ref[...] = gbuf[CHUNK:CHUNK + CONV_PAD, :]


def _conv_branch(u, state_pad, conv_w, conv_b, ln_g, ln_b, seq_of_chunk, chunk_start):
    t = u.shape[0]
    ch = conv_w.shape[1]
    nseq = state_pad.shape[0]
    nchunks = t // CHUNK
    grid_spec = pltpu.PrefetchScalarGridSpec(
        num_scalar_prefetch=2,
        grid=(nchunks,),
        in_specs=[
            pl.BlockSpec((CHUNK, ch), lambda i, s, f: (i, 0)),
            pl.BlockSpec((CHUNK, ch), lambda i, s, f: (i, 1)),
            pl.BlockSpec((None, CONV_PAD, ch), lambda i, s, f: (s[i], 0, 0)),
            pl.BlockSpec((CONV_WIDTH, ch), lambda i, s, f: (0, 0)),
            pl.BlockSpec((1, ch), lambda i, s, f: (0, 0)),
            pl.BlockSpec((1, ch), lambda i, s, f: (0, 0)),
            pl.BlockSpec((1, ch), lambda i, s, f: (0, 0)),
        ],
        out_specs=[
            pl.BlockSpec((CHUNK, ch), lambda i, s, f: (i, 0)),
            pl.BlockSpec((None, CONV_PAD, ch), lambda i, s, f: (s[i], 0, 0)),
        ],
        scratch_shapes=[pltpu.VMEM((CONV_PAD + CHUNK, ch), F32), pltpu.VMEM((CHUNK, ch), F32)],
    )
    return pl.pallas_call(
        functools.partial(_conv_kernel, cblk=min(256, ch)),
        out_shape=[jax.ShapeDtypeStruct((t, ch), BF16),
                   jax.ShapeDtypeStruct((nseq, CONV_PAD, ch), F32)],
        grid_spec=grid_spec,
        compiler_params=_cparams(("arbitrary",)),
        name="conv_branch",
    )(seq_of_chunk, chunk_start, u, u, state_pad, conv_w, conv_b.reshape(1, ch),
      ln_g.reshape(1, ch), ln_b.reshape(1, ch))


def _pool_kernel(seq_ref, start_ref, pos_ref, p_ref, st_ref, w_ref, b_ref, sc_ref,
                 o_ref, sto_ref, pbuf):
    i = pl.program_id(0)
    gch = w_ref.shape[1]

    @pl.when(start_ref[i] == 1)
    def _():
        pbuf[0:POOL_PAD, :] = st_ref[...]

    @pl.when(start_ref[i] == 0)
    def _():
        pbuf[0:POOL_PAD, :] = pbuf[CHUNK:CHUNK + POOL_PAD, :]

    pbuf[POOL_PAD:POOL_PAD + CHUNK, :] = p_ref[...]

    pos = pos_ref[i] + lax.broadcasted_iota(jnp.int32, (CHUNK, 1), 0)
    for gi, win in enumerate(POOL_WINDOWS):
        cols = slice(gi * gch, (gi + 1) * gch)
        cur = pbuf[POOL_PAD:POOL_PAD + CHUNK, cols]
        wsum = cur
        for j in range(1, win):
            wsum = wsum + pbuf[POOL_PAD - j:POOL_PAD - j + CHUNK, cols]
        count = jnp.minimum(pos + 1, win).astype(F32)
        diff = wsum / count - cur
        y = jnp.dot(diff.astype(BF16), w_ref[gi], preferred_element_type=F32) + b_ref[gi]
        o_ref[:, cols] = (y * sc_ref[:, cols]).astype(o_ref.dtype)
    sto_ref[...] = pbuf[CHUNK:CHUNK + POOL_PAD, :]


def _pool_branch(u, col_block, state_pad, pool_w, pool_b, pool_scale,
                 seq_of_chunk, chunk_start, chunk_pos):
    t = u.shape[0]
    ng, gch, _ = pool_w.shape
    ch = ng * gch
    nseq = state_pad.shape[0]
    grid_spec = pltpu.PrefetchScalarGridSpec(
        num_scalar_prefetch=3,
        grid=(t // CHUNK,),
        in_specs=[
            pl.BlockSpec((CHUNK, ch), lambda i, s, f, p: (i, col_block)),
            pl.BlockSpec((None, POOL_PAD, ch), lambda i, s, f, p: (s[i], 0, 0)),
            pl.BlockSpec((ng, gch, gch), lambda i, s, f, p: (0, 0, 0)),
            pl.BlockSpec((ng, 1, gch), lambda i, s, f, p: (0, 0, 0)),
            pl.BlockSpec((1, ch), lambda i, s, f, p: (0, 0)),
        ],
        out_specs=[
            pl.BlockSpec((CHUNK, ch), lambda i, s, f, p: (i, 0)),
            pl.BlockSpec((None, POOL_PAD, ch), lambda i, s, f, p: (s[i], 0, 0)),
        ],
        scratch_shapes=[pltpu.VMEM((POOL_PAD + CHUNK, ch), F32)],
    )
    return pl.pallas_call(
        _pool_kernel,
        out_shape=[jax.ShapeDtypeStruct((t, ch), BF16),
                   jax.ShapeDtypeStruct((nseq, POOL_PAD, ch), F32)],
        grid_spec=grid_spec,
        compiler_params=_cparams(("arbitrary",)),
        name="pool_branch",
    )(seq_of_chunk, chunk_start, chunk_pos, u, state_pad, pool_w.astype(BF16),
      pool_b.reshape(ng, 1, gch), pool_scale.reshape(1, ch))


def _resid_router_kernel(x_ref, y_ref, g1_ref, g2_ref, gt_ref, sc_ref, sh_ref, wr_ref, br_ref,
                         x1_ref, h_ref, idx_ref, gate_ref, cnt_ref, *, nch):
    for c in range(nch):
        rows = slice(c * CHUNK, (c + 1) * CHUNK)
        x1 = x_ref[rows, :] + gt_ref[c] * (_rms(y_ref[rows, :]) * g1_ref[...])
        x1_ref[rows, :] = x1
        h_ref[rows, :] = (_rms(x1) * g2_ref[...]) * (1.0 + sc_ref[c]) + sh_ref[c]

    logits = jnp.dot(h_ref[...].astype(BF16), wr_ref[...], preferred_element_type=F32) + br_ref[...]
    lane = lax.broadcasted_iota(jnp.int32, logits.shape, 1).astype(F32)
    vals, idxs = [], []
    for _ in range(TOP_K):
        m = jnp.max(logits, axis=-1, keepdims=True)
        sel = jnp.min(jnp.where(logits == m, lane, float(LANES)), axis=-1, keepdims=True)
        vals.append(m)
        idxs.append(sel)
        logits = jnp.where(lane == sel, -jnp.inf, logits)
    exps = [jnp.exp(v - vals[0]) for v in vals]
    den = exps[0] + exps[1] + exps[2] + exps[3]

    @pl.when(pl.program_id(0) == 0)
    def _():
        cnt_ref[...] = jnp.zeros_like(cnt_ref)

    tr = logits.shape[0]
    onehot = jnp.zeros(logits.shape, F32)
    for k in range(TOP_K):
        onehot = onehot + jnp.where(lane == idxs[k], 1.0, 0.0)
    rr = lax.broadcasted_iota(jnp.int32, (tr, tr), 0)
    cc = lax.broadcasted_iota(jnp.int32, (tr, tr), 1)
    lower = jnp.where(cc < rr, 1.0, 0.0).astype(BF16)
    before = jnp.dot(lower, onehot.astype(BF16), preferred_element_type=F32) + cnt_ref[...]
    cnt_ref[...] += jnp.sum(onehot, axis=0, keepdims=True)

    gate_out = jnp.zeros(logits.shape, F32)
    idx_out = jnp.zeros(logits.shape, F32)
    for k in range(TOP_K):
        rank = jnp.sum(jnp.where(lane == idxs[k], before, 0.0), axis=-1, keepdims=True)
        gate_out = jnp.where(lane == float(k), exps[k] / den, gate_out)
        idx_out = jnp.where(lane == float(k), idxs[k], idx_out)
        idx_out = jnp.where(lane == float(TOP_K + k), rank, idx_out)
    gate_ref[...] = gate_out
    idx_ref[...] = idx_out.astype(jnp.int32)


def _resid_router(x, y, g1, g2, modc, gt_col, sc_col, sh_col, w_router, b_router):
    t, d = x.shape
    ne = w_router.shape[1]
    nch = ROW_TILE // CHUNK
    wr = jnp.zeros((d, LANES), BF16).at[:, :ne].set(w_router.astype(BF16))
    br = jnp.full((1, LANES), -1e30, F32).at[0, :ne].set(b_router.astype(F32))
    mspec = lambda col: pl.BlockSpec((nch, 1, d), lambda i: (i, 0, col))
    row = pl.BlockSpec((ROW_TILE, d), lambda i: (i, 0))
    vec = pl.BlockSpec((1, d), lambda i: (0, 0))
    small = pl.BlockSpec((ROW_TILE, LANES), lambda i: (i, 0))
    return pl.pallas_call(
        functools.partial(_resid_router_kernel, nch=nch),
        out_shape=[jax.ShapeDtypeStruct((t, d), F32), jax.ShapeDtypeStruct((t, d), F32),
                   jax.ShapeDtypeStruct((t, LANES), jnp.int32),
                   jax.ShapeDtypeStruct((t, LANES), F32),
                   jax.ShapeDtypeStruct((1, LANES), F32)],
        grid=(t // ROW_TILE,),
        in_specs=[row, row, vec, vec, mspec(gt_col), mspec(sc_col), mspec(sh_col),
                  pl.BlockSpec((d, LANES), lambda i: (0, 0)),
                  pl.BlockSpec((1, LANES), lambda i: (0, 0))],
        out_specs=[row, row, small, small, pl.BlockSpec((1, LANES), lambda i: (0, 0))],
        compiler_params=_cparams(("arbitrary",)),
        name="resid_router",
    )(x, y, g1.reshape(1, d), g2.reshape(1, d), modc, modc, modc, wr, br)


def _moe_gather_kernel(tok_ref, nv_ref, h3_hbm, o_ref, buf, tmp, sem, *, tm, nsub):
    i = pl.program_id(0)
    nv = nv_ref[0]

    def issue(tile, slot):
        def body(r, carry):
            tok = tok_ref[tile * tm + r]
            pltpu.make_async_copy(h3_hbm.at[tok], buf.at[slot * tm + r, pl.ds(0, nsub)],
                                  sem.at[slot]).start()
            return carry
        lax.fori_loop(0, tm, body, 0)

    @pl.when(i == 0)
    def _():
        issue(0, 0)

    @pl.when(i + 1 < nv)
    def _():
        issue(i + 1, (i + 1) % 2)

    @pl.when(i < nv)
    def _():
        slot = i % 2
        base = pl.multiple_of(slot * tm, tm)
        pltpu.make_async_copy(h3_hbm.at[pl.ds(0, tm)], buf.at[pl.ds(base, tm), pl.ds(0, nsub)],
                              sem.at[slot]).wait()
        for kk in range(nsub):
            tmp[:, kk * LANES:(kk + 1) * LANES] = buf[pl.ds(base, tm), kk, :]
        o_ref[...] = tmp[...].astype(o_ref.dtype)

    @pl.when(i >= nv)
    def _():
        o_ref[...] = jnp.zeros_like(o_ref)


def _moe_gather(slot_token, n_valid, h3, n_tiles):
    t, nsub, _ = h3.shape
    tm = MOE_TM
    grid_spec = pltpu.PrefetchScalarGridSpec(
        num_scalar_prefetch=2,
        grid=(n_tiles,),
        in_specs=[pl.BlockSpec(memory_space=pl.ANY)],
        out_specs=pl.BlockSpec((tm, nsub * LANES), lambda i, tok, nv: (i, 0)),
        scratch_shapes=[pltpu.VMEM((2 * tm, nsub + SLAB_PAD, LANES), F32),
                        pltpu.VMEM((tm, nsub * LANES), F32), pltpu.SemaphoreType.DMA((2,))],
    )
    return pl.pallas_call(
        functools.partial(_moe_gather_kernel, tm=tm, nsub=nsub),
        out_shape=jax.ShapeDtypeStruct((n_tiles * tm, nsub * LANES), BF16),
        grid_spec=grid_spec,
        compiler_params=_cparams(("arbitrary",)),
        name="moe_gather",
    )(slot_token, n_valid, h3)


def _moe_up_kernel(te_ref, first_ref, nxt_ref, rows_ref, x_ref, w_hbm, bg_ref, bl_ref, o_ref,
                   stage, w_bf, sem, *, layer, tn, nj):
    j = pl.program_id(0)
    i = pl.program_id(1)

    def copies(e, jj):
        glu = pl.ds(pl.multiple_of(jj * tn, tn), tn)
        lin = pl.ds(pl.multiple_of((nj + jj) * tn, tn), tn)
        return (pltpu.make_async_copy(w_hbm.at[layer, e, :, glu], stage.at[0], sem.at[0]),
                pltpu.make_async_copy(w_hbm.at[layer, e, :, lin], stage.at[1], sem.at[1]))

    def start(e, jj):
        for c in copies(e, jj):
            c.start()

    @pl.when(jnp.logical_and(i == 0, j == 0))
    def _():
        start(te_ref[0], 0)

    @pl.when(first_ref[i] == 1)
    def _():
        for c in copies(te_ref[i], j):
            c.wait()
        w_bf[...] = stage[...].astype(BF16)
        nxt = nxt_ref[i]

        @pl.when(nxt >= 0)
        def _():
            start(nxt, j)

        @pl.when(jnp.logical_and(nxt < 0, j + 1 < nj))
        def _():
            start(te_ref[0], j + 1)

    def compute(rows):
        x = x_ref[rows, :]
        hg = jnp.dot(x, w_bf[0], preferred_element_type=F32) + bg_ref[...]
        hl = jnp.dot(x, w_bf[1], preferred_element_type=F32) + bl_ref[...]
        x_glu = jnp.minimum(hg, SWIGLU_LIMIT)
        x_lin = jnp.clip(hl, -SWIGLU_LIMIT, SWIGLU_LIMIT)
        act = x_glu * jax.nn.sigmoid(SWIGLU_ALPHA * x_glu) * (x_lin + 1.0)
        o_ref[rows, :] = act.astype(o_ref.dtype)

    tm = x_ref.shape[0]
    nrows = rows_ref[i]

    @pl.when(nrows > tm // 2)
    def _():
        compute(slice(0, tm))

    @pl.when(nrows <= tm // 2)
    def _():
        o_ref[tm // 2:tm, :] = jnp.zeros((tm - tm // 2, o_ref.shape[1]), o_ref.dtype)

    @pl.when(jnp.logical_and(nrows > 0, nrows <= tm // 2))
    def _():
        compute(slice(0, tm // 2))

    @pl.when(nrows == 0)
    def _():
        o_ref[0:tm // 2, :] = jnp.zeros((tm // 2, o_ref.shape[1]), o_ref.dtype)


def _moe_up(x_sorted, w_up, b_up, layer, tile_expert, tile_first, tile_next, tile_rows, tn=512):
    n_pad, d = x_sorted.shape
    _, ne, _, f2 = w_up.shape
    f = f2 // 2
    tm = MOE_TM
    n_tiles = n_pad // tm
    tn = min(tn, f)
    nj = f // tn
    grid_spec = pltpu.PrefetchScalarGridSpec(
        num_scalar_prefetch=4,
        grid=(nj, n_tiles),
        in_specs=[
            pl.BlockSpec((tm, d), lambda j, i, te, fi, nx, nv: (i, 0)),
            pl.BlockSpec(memory_space=pl.ANY),
            pl.BlockSpec((None, None, 1, tn), lambda j, i, te, fi, nx, nv: (layer, te[i], 0, j)),
            pl.BlockSpec((None, None, 1, tn), lambda j, i, te, fi, nx, nv: (layer, te[i], 0, nj + j)),
        ],
        out_specs=pl.BlockSpec((tm, tn), lambda j, i, te, fi, nx, nv: (i, j)),
        scratch_shapes=[pltpu.VMEM((2, d, tn), F32), pltpu.VMEM((2, d, tn), BF16),
                        pltpu.SemaphoreType.DMA((2,))],
    )
    b4 = b_up.reshape(b_up.shape[0], ne, 1, f2)
    return pl.pallas_call(
        functools.partial(_moe_up_kernel, layer=layer, tn=tn, nj=nj),
        out_shape=jax.ShapeDtypeStruct((n_pad, f), BF16),
        grid_spec=grid_spec,
        compiler_params=_cparams(("arbitrary", "arbitrary")),
        name="moe_up",
    )(tile_expert, tile_first, tile_next, tile_rows, x_sorted, w_up, b4, b4)


def _moe_down_kernel(te_ref, first_ref, nxt_ref, rows_ref, a_ref, w_hbm, b_ref, o_ref,
                     stage, w_bf, sem, *, layer, tn, nj):
    j = pl.program_id(0)
    i = pl.program_id(1)

    def copy(e, jj):
        cols = pl.ds(pl.multiple_of(jj * tn, tn), tn)
        return pltpu.make_async_copy(w_hbm.at[layer, e, :, cols], stage, sem.at[0])

    @pl.when(jnp.logical_and(i == 0, j == 0))
    def _():
        copy(te_ref[0], 0).start()

    @pl.when(first_ref[i] == 1)
    def _():
        copy(te_ref[i], j).wait()
        w_bf[...] = stage[...].astype(BF16)
        nxt = nxt_ref[i]

        @pl.when(nxt >= 0)
        def _():
            copy(nxt, j).start()

        @pl.when(jnp.logical_and(nxt < 0, j + 1 < nj))
        def _():
            copy(te_ref[0], j + 1).start()

    def compute(rows):
        o_ref[rows, :] = jnp.dot(a_ref[rows, :], w_bf[...], preferred_element_type=F32) + b_ref[...]

    tm = a_ref.shape[0]
    nrows = rows_ref[i]

    @pl.when(nrows > tm // 2)
    def _():
        compute(slice(0, tm))

    @pl.when(nrows <= tm // 2)
    def _():
        o_ref[tm // 2:tm, :] = jnp.zeros((tm - tm // 2, o_ref.shape[1]), o_ref.dtype)

    @pl.when(jnp.logical_and(nrows > 0, nrows <= tm // 2))
    def _():
        compute(slice(0, tm // 2))

    @pl.when(nrows == 0)
    def _():
        o_ref[0:tm // 2, :] = jnp.zeros((tm // 2, o_ref.shape[1]), o_ref.dtype)


def _moe_down(act, w_down, b_down, layer, tile_expert, tile_first, tile_next, tile_rows, tn=1024):
    n_pad, f = act.shape
    _, ne, _, d = w_down.shape
    tm = MOE_TM
    n_tiles = n_pad // tm
    tn = min(tn, d)
    nj = d // tn
    grid_spec = pltpu.PrefetchScalarGridSpec(
        num_scalar_prefetch=4,
        grid=(nj, n_tiles),
        in_specs=[
            pl.BlockSpec((tm, f), lambda j, i, te, fi, nx, nv: (i, 0)),
            pl.BlockSpec(memory_space=pl.ANY),
            pl.BlockSpec((None, None, 1, tn), lambda j, i, te, fi, nx, nv: (layer, te[i], 0, j)),
        ],
        out_specs=pl.BlockSpec((tm, tn), lambda j, i, te, fi, nx, nv: (i, j)),
        scratch_shapes=[pltpu.VMEM((f, tn), F32), pltpu.VMEM((f, tn), BF16),
                        pltpu.SemaphoreType.DMA((1,))],
    )
    return pl.pallas_call(
        functools.partial(_moe_down_kernel, layer=layer, tn=tn, nj=nj),
        out_shape=jax.ShapeDtypeStruct((n_pad, d), F32),
        grid_spec=grid_spec,
        compiler_params=_cparams(("arbitrary", "arbitrary")),
        name="moe_down",
    )(tile_expert, tile_first, tile_next, tile_rows, act, w_down,
      b_down.reshape(b_down.shape[0], ne, 1, d))


def _moe_combine_kernel(slot_ref, y3_hbm, x_ref, gate_ref, g_ref, gt_ref, o_ref,
                        buf, sum3, ybuf, sem, *, tt, nt, nsub):
    i = pl.program_id(0)
    rows_per_slot = TOP_K * tt

    def issue(tile, s):
        def body(r, carry):
            for k in range(TOP_K):
                src = slot_ref[(tile * tt + r) * TOP_K + k]
                pltpu.make_async_copy(y3_hbm.at[src],
                                      buf.at[s * rows_per_slot + k * tt + r, pl.ds(0, nsub)],
                                      sem.at[s]).start()
            return carry
        lax.fori_loop(0, tt, body, 0)

    @pl.when(i == 0)
    def _():
        issue(0, 0)

    @pl.when(i + 1 < nt)
    def _():
        issue(i + 1, (i + 1) % 2)

    s = i % 2
    base = pl.multiple_of(s * rows_per_slot, rows_per_slot)
    pltpu.make_async_copy(y3_hbm.at[pl.ds(0, rows_per_slot)],
                          buf.at[pl.ds(base, rows_per_slot), pl.ds(0, nsub)], sem.at[s]).wait()

    def token_body(r, carry):
        acc = gate_ref[r, 0:1, :] * buf[base + r, pl.ds(0, nsub), :]
        for k in range(1, TOP_K):
            acc = acc + gate_ref[r, k:k + 1, :] * buf[base + k * tt + r, pl.ds(0, nsub), :]
        sum3[r, pl.ds(0, nsub), :] = acc
        return carry
    lax.fori_loop(0, tt, token_body, 0)

    for kk in range(nsub):
        ybuf[:, kk * LANES:(kk + 1) * LANES] = sum3[:, kk, :]
    for c in range(tt // CHUNK):
        rows = slice(c * CHUNK, (c + 1) * CHUNK)
        o_ref[rows, :] = x_ref[rows, :] + gt_ref[c] * (_rms(ybuf[rows, :]) * g_ref[...])


def _moe_combine(slot, y3, x1, gates_pad, g, modc, gt_col):
    t, d = x1.shape
    nsub = d // LANES
    tt = COMBINE_TT
    nt = t // tt
    nch = tt // CHUNK
    gates_rep = jnp.broadcast_to(gates_pad[:, :TOP_K, None], (t, TOP_K, LANES))
    grid_spec = pltpu.PrefetchScalarGridSpec(
        num_scalar_prefetch=1,
        grid=(nt,),
        in_specs=[
            pl.BlockSpec(memory_space=pl.ANY),
            pl.BlockSpec((tt, d), lambda i, sl: (i, 0)),
            pl.BlockSpec((tt, TOP_K, LANES), lambda i, sl: (i, 0, 0)),
            pl.BlockSpec((1, d), lambda i, sl: (0, 0)),
            pl.BlockSpec((nch, 1, d), lambda i, sl: (i, 0, gt_col)),
        ],
        out_specs=pl.BlockSpec((tt, d), lambda i, sl: (i, 0)),
        scratch_shapes=[pltpu.VMEM((2 * TOP_K * tt, nsub + SLAB_PAD, LANES), F32),
                        pltpu.VMEM((tt, nsub + SLAB_PAD, LANES), F32),
                        pltpu.VMEM((tt, d), F32), pltpu.SemaphoreType.DMA((2,))],
    )
    return pl.pallas_call(
        functools.partial(_moe_combine_kernel, tt=tt, nt=nt, nsub=nsub),
        out_shape=jax.ShapeDtypeStruct((t, d), F32),
        grid_spec=grid_spec,
        compiler_params=_cparams(("arbitrary",)),
        name="moe_combine",
    )(slot, y3, x1, gates_rep, g.reshape(1, d), modc)


def _moe_layer(x1, h, route, counts, gates_pad, layer, w_up, b_up, w_down, b_down, g_post, modc,
               gt_col):
    t, d = h.shape
    ne = w_up.shape[1]
    tm = MOE_TM
    n_assign = t * TOP_K
    n_tiles = n_assign // tm + ne

    flat_e = route[:, :TOP_K].reshape(-1)
    rank = route[:, TOP_K:2 * TOP_K].reshape(-1)
    padded = (counts + tm - 1) // tm * tm
    pad_end = jnp.cumsum(padded)
    pad_start = pad_end - padded
    slot = (pad_start[flat_e] + rank).astype(jnp.int32)
    slot_token = jnp.zeros((n_tiles * tm,), jnp.int32).at[slot].set(
        jnp.arange(n_assign, dtype=jnp.int32) // TOP_K)
    n_valid = (pad_end[-1] // tm).astype(jnp.int32)
    tile_ids = jnp.arange(n_tiles, dtype=jnp.int32)
    tile_start = jnp.minimum(tile_ids, n_valid - 1) * tm
    tile_expert = jnp.sum((pad_end[None, :] <= tile_start[:, None]).astype(jnp.int32), axis=1)
    tile_expert = jnp.minimum(tile_expert, ne - 1)
    tile_rows = jnp.clip(counts[tile_expert] - (tile_start - pad_start[tile_expert]), 0, tm)
    tile_rows = jnp.where(tile_ids < n_valid, tile_rows, 0).astype(jnp.int32)
    tile_first = jnp.concatenate([jnp.ones((1,), jnp.int32),
                                  (tile_expert[1:] != tile_expert[:-1]).astype(jnp.int32)])
    first_pos = jnp.where(tile_first == 1, tile_ids, n_tiles)
    next_pos = jnp.concatenate([lax.cummin(first_pos, reverse=True)[1:],
                                jnp.full((1,), n_tiles, jnp.int32)])
    tile_next = jnp.where(next_pos < n_tiles, tile_expert[jnp.minimum(next_pos, n_tiles - 1)], -1)
    tile_next = tile_next.astype(jnp.int32)

    h3 = h.reshape(t, d // LANES, LANES)
    nv = n_valid.reshape(1)
    x_sorted = _moe_gather(slot_token, nv, h3, n_tiles)
    act = _moe_up(x_sorted, w_up, b_up, layer, tile_expert, tile_first, tile_next, tile_rows)
    y_sorted = _moe_down(act, w_down, b_down, layer, tile_expert, tile_first, tile_next, tile_rows)
    y3 = y_sorted.reshape(n_tiles * tm, d // LANES, LANES)
    return _moe_combine(slot, y3, x1, gates_pad, g_post, modc, gt_col)


def _log_gates(z):
    sp = jnp.log(1.0 + jnp.exp(-jnp.abs(z)))
    return jnp.minimum(z, 0.0) - sp, jnp.minimum(-z, 0.0) - sp


def _split_bf16(x):
    hi = x.astype(BF16)
    return hi, (x - hi.astype(F32)).astype(BF16)


def _sb_window(q, k_ref, v_ref, k0, nblk, carry, qpos, klim, tri):
    gates = _sb_scores(q, k_ref, k0, nblk, qpos, klim)
    return _sb_weights(gates, _sb_suffix(gates, tri), carry, v_ref, k0, nblk)


def _sb_scores(q, k_ref, k0, nblk, qpos, klim):
    tb = SB_BLOCK
    tq = q.shape[0]
    col = lax.broadcasted_iota(jnp.int32, (tq, tb), 1)
    lim = jnp.minimum(qpos, klim)
    log_beta, log_keep, masks = [], [], []
    for b in range(nblk):
        kt = k_ref[pl.ds(pl.multiple_of(k0 + b * tb, tb), tb), :]
        z = lax.dot_general(q, kt, (((1,), (1,)), ((), ())), preferred_element_type=F32) * SB_SCALE
        lb, lk = _log_gates(z)
        mask = (k0 + b * tb + col) < lim
        log_beta.append(lb)
        log_keep.append(jnp.where(mask, lk, 0.0))
        masks.append(mask)
    return log_beta, log_keep, masks


def _sb_suffix(gates, tri):
    hi, lo = _split_bf16(jnp.concatenate(gates[1], axis=0))
    return jnp.dot(jnp.concatenate([hi, lo], axis=0), tri, preferred_element_type=F32)


def _sb_weights(gates, suf, carry, v_ref, k0, nblk):
    log_beta, log_keep, masks = gates
    tb = SB_BLOCK
    tq = log_beta[0].shape[0]
    a_blocks = [None] * nblk
    for b in reversed(range(nblk)):
        s = suf[b * tq:(b + 1) * tq] + suf[(nblk + b) * tq:(nblk + b + 1) * tq]
        a = jnp.exp(log_beta[b] + (s - log_keep[b]) + carry)
        a_blocks[b] = jnp.where(masks[b], a, 0.0).astype(BF16)
        carry = carry + s[:, 0:1]
    vw = v_ref[pl.ds(pl.multiple_of(k0, tb), nblk * tb), :]
    out = jnp.dot(jnp.concatenate(a_blocks, axis=1), vw, preferred_element_type=F32)
    return out, carry


def _alive(carry):
    return (jnp.max(carry) > UNDERFLOW).astype(jnp.int32)


def _tri(tk):
    r = lax.broadcasted_iota(jnp.int32, (tk, tk), 0)
    c = lax.broadcasted_iota(jnp.int32, (tk, tk), 1)
    return jnp.where(r >= c, 1.0, 0.0).astype(BF16)


def _strict_lower(tq, tk):
    r = lax.broadcasted_iota(jnp.int32, (tq, tk), 0)
    c = lax.broadcasted_iota(jnp.int32, (tq, tk), 1)
    return c < r


def _attn_prompt_kernel(q_ref, k_ref, v_ref, o_ref, kb_ref, vb_ref, *, seq, nblk, ntile):
    tb = SB_BLOCK
    kb_ref[...] = k_ref[...].astype(BF16)
    vb_ref[...] = v_ref[...].astype(BF16)
    tri = _tri(tb)
    row = lax.broadcasted_iota(jnp.int32, (tb, 1), 0)

    def tile_setup(qi):
        r0 = pl.multiple_of(qi * tb, tb)
        q = q_ref[pl.ds(r0, tb), :].astype(BF16)
        return r0, q, r0 + row, jnp.maximum(qi - (nblk - 1), 0)

    def finish(r0, q, qpos, ws, acc, carry):
        def cond(st):
            return jnp.logical_and(st[0] > 0, st[1] > 0)

        def body(st):
            ws, _, acc, carry = st
            ws2 = jnp.maximum(ws - nblk, 0)
            out, carry = _sb_window(q, kb_ref, vb_ref, ws2 * tb, nblk, carry, qpos, ws * tb, tri)
            return ws2, _alive(carry), acc + out, carry

        _, _, acc, _ = lax.while_loop(cond, body, (ws, _alive(carry), acc, carry))
        o_ref[pl.ds(r0, tb), :] = acc.astype(o_ref.dtype)

    def group_body(p, unused):
        tiles = [tile_setup(ntile * p + u) for u in range(ntile)]
        gates = [_sb_scores(q, kb_ref, ws * tb, nblk, qpos, BIG_POS) for _, q, qpos, ws in tiles]
        sufs = [_sb_suffix(g, tri) for g in gates]
        zero = jnp.zeros((tb, 1), F32)
        outs = [_sb_weights(g, s, zero, vb_ref, tl[3] * tb, nblk)
                for g, s, tl in zip(gates, sufs, tiles)]
        for (r0, q, qpos, ws), (acc, carry) in zip(tiles, outs):
            finish(r0, q, qpos, ws, acc, carry)
        return unused

    lax.fori_loop(0, seq // (ntile * tb), group_body, 0)


def _attn_prompt(q, k, v, nb, seq):
    d = q.shape[1]
    nh = d // HEAD_DIM
    nblk = min(SB_WINDOW_BLOCKS, seq // SB_BLOCK)
    ntile = math.gcd(SB_TILES_PER_STEP, seq // SB_BLOCK)
    blk = pl.BlockSpec((seq, HEAD_DIM), lambda b, h: (b, h))
    return pl.pallas_call(
        functools.partial(_attn_prompt_kernel, seq=seq, nblk=nblk, ntile=ntile),
        out_shape=jax.ShapeDtypeStruct((nb * seq, d), BF16),
        grid=(nb, nh),
        in_specs=[blk, blk, blk],
        out_specs=blk,
        scratch_shapes=[pltpu.VMEM((seq, HEAD_DIM), BF16), pltpu.VMEM((seq, HEAD_DIM), BF16)],
        compiler_params=_cparams(("arbitrary", "arbitrary")),
        name="attn_prompt",
    )(q, k, v)


def _attn_sample_kernel(q_ref, kn_ref, vn_ref, ck_hbm, cv_hbm, o_ref,
                        kbuf, vbuf, lb_s, lk_s, hl_s, suf_s, acc_ref, carry_ref, sem,
                        *, nh, lq, tkc, nkt):
    b = pl.program_id(0)
    nrow = nh * lq

    def copies(t, slot):
        k0 = pl.multiple_of((nkt - 1 - t) * tkc, tkc)
        dst = pl.ds(pl.multiple_of(slot * tkc, tkc), tkc)
        return (pltpu.make_async_copy(ck_hbm.at[b, pl.ds(k0, tkc)], kbuf.at[dst], sem.at[0, slot]),
                pltpu.make_async_copy(cv_hbm.at[b, pl.ds(k0, tkc)], vbuf.at[dst], sem.at[1, slot]))

    def start(t, slot):
        for c in copies(t, slot):
            c.start()

    def wait(t, slot):
        for c in copies(t, slot):
            c.wait()

    def heads_tile(get_k, get_v, tk, mask, first):
        tri = _tri(tk)
        for h in range(nh):
            cols = slice(h * HEAD_DIM, (h + 1) * HEAD_DIM)
            rows = slice(h * lq, (h + 1) * lq)
            z = lax.dot_general(q_ref[:, cols].astype(BF16), get_k(h), (((1,), (1,)), ((), ())),
                                preferred_element_type=F32) * SB_SCALE
            lb, lk = _log_gates(z)
            if mask is not None:
                lk = jnp.where(mask, lk, 0.0)
            hi, lo = _split_bf16(lk)
            lb_s[rows, 0:tk] = lb
            lk_s[rows, 0:tk] = lk
            hl_s[rows, 0:tk] = hi
            hl_s[nrow + h * lq:nrow + (h + 1) * lq, 0:tk] = lo
        suf = jnp.dot(hl_s[:, 0:tk], tri, preferred_element_type=F32)
        suf_s[:, 0:tk] = suf[0:nrow] + suf[nrow:2 * nrow]
        for h in range(nh):
            cols = slice(h * HEAD_DIM, (h + 1) * HEAD_DIM)
            rows = slice(h * lq, (h + 1) * lq)
            s = suf_s[rows, 0:tk]
            carry = jnp.zeros((lq, 1), F32) if first else carry_ref[:, cols][:, 0:1]
            a = jnp.exp(lb_s[rows, 0:tk] + (s - lk_s[rows, 0:tk]) + carry)
            if mask is not None:
                a = jnp.where(mask, a, 0.0)
            out = jnp.dot(a.astype(BF16), get_v(h), preferred_element_type=F32)
            if first:
                acc_ref[:, cols] = out
            else:
                acc_ref[:, cols] += out
            carry_ref[:, cols] = jnp.broadcast_to(carry + s[:, 0:1], (lq, HEAD_DIM))

    start(0, 0)
    heads_tile(lambda h: kn_ref[:, h * HEAD_DIM:(h + 1) * HEAD_DIM].astype(BF16),
               lambda h: vn_ref[:, h * HEAD_DIM:(h + 1) * HEAD_DIM].astype(BF16),
               lq, _strict_lower(lq, lq), True)

    def cond(st):
        return jnp.logical_and(st[0] < nkt, st[1] > 0)

    def body(st):
        t = st[0]
        slot = t % 2
        wait(t, slot)

        @pl.when(t + 1 < nkt)
        def _():
            start(t + 1, 1 - slot)

        rows = pl.ds(pl.multiple_of(slot * tkc, tkc), tkc)
        heads_tile(lambda h: kbuf[rows, h, :].astype(BF16), lambda h: vbuf[rows, h, :].astype(BF16),
                   tkc, None, False)
        return t + 1, _alive(carry_ref[...])

    t_end, _ = lax.while_loop(cond, body, (jnp.int32(0), _alive(carry_ref[...])))

    @pl.when(t_end < nkt)
    def _():
        wait(t_end, t_end % 2)

    o_ref[...] = acc_ref[...].astype(o_ref.dtype)


def _attn_sample(q, k, v, cache_k, cache_v, row_blk0, lq):
    nb, past, nh, hd = cache_k.shape
    d = nh * hd
    tkc = min(SAMPLE_TKC, past)
    assert past % tkc == 0
    nkt = past // tkc
    nrow = nh * lq
    qspec = pl.BlockSpec((lq, d), lambda b: (row_blk0 + b, 0))
    new = pl.BlockSpec((lq, d), lambda b: (b, 0))
    hbm = pl.BlockSpec(memory_space=pl.ANY)
    return pl.pallas_call(
        functools.partial(_attn_sample_kernel, nh=nh, lq=lq, tkc=tkc, nkt=nkt),
        out_shape=jax.ShapeDtypeStruct((nb * lq, d), BF16),
        grid=(nb,),
        in_specs=[qspec, new, new, hbm, hbm],
        out_specs=pl.BlockSpec((lq, d), lambda b: (b, 0)),
        scratch_shapes=[
            pltpu.VMEM((2 * tkc, nh, hd), F32), pltpu.VMEM((2 * tkc, nh, hd), F32),
            pltpu.VMEM((nrow, tkc), F32), pltpu.VMEM((nrow, tkc), F32),
            pltpu.VMEM((2 * nrow, tkc), BF16), pltpu.VMEM((nrow, tkc), F32),
            pltpu.VMEM((lq, d), F32), pltpu.VMEM((lq, d), F32),
            pltpu.SemaphoreType.DMA((2, 2)),
        ],
        compiler_params=_cparams(("arbitrary",)),
        name="attn_sample",
    )(q, k, v, cache_k, cache_v)


def kernel(x_prompt, x_sample, c_prompt, c_sample, state_conv, state_pool, cache_k, cache_v, norm_g, w_mod, b_mod, w_in_ab, b_in_ab, conv_w, conv_b, ln_g, ln_b, pool_w, pool_b, pool_scale, w_out_ab, w_qkv, w_o, w_router, b_router, w_up, b_up, w_down, b_down):
    nbp, lp, d = x_prompt.shape
    nbs, ls, _ = x_sample.shape
    past = cache_k.shape[1]
    conv_ch = conv_w.shape[1]
    pool_ch = pool_scale.shape[0]
    nh = d // HEAD_DIM
    tp, ts = nbp * lp, nbs * ls
    t = tp + ts
    assert lp % CHUNK == 0 and ls == CHUNK and t % ROW_TILE == 0 and tp % COMBINE_TT == 0
    assert ls >= CONV_BUF and lp >= CONV_BUF and conv_ch == pool_ch and d % conv_ch == 0

    cpp = lp // CHUNK
    seq_np = np.concatenate([np.repeat(np.arange(nbp), cpp), nbp + np.arange(nbs)]).astype(np.int32)
    start_np = np.concatenate([np.tile(np.arange(cpp) == 0, nbp), np.ones(nbs, bool)]).astype(np.int32)
    pos_np = np.concatenate([np.tile(np.arange(cpp) * CHUNK, nbp), np.full(nbs, past)]).astype(np.int32)
    seq_of_chunk, chunk_start, chunk_pos = jnp.asarray(seq_np), jnp.asarray(start_np), jnp.asarray(pos_np)

    nseq = nbp + nbs
    nseq_pad = -(-nseq // 16) * 16
    c_all = jnp.zeros((nseq_pad, d), F32).at[:nbp].set(c_prompt).at[nbp:nseq].set(c_sample)
    mod = _modulation(c_all, w_mod, b_mod)
    modc_all = mod[:, seq_of_chunk, :][:, :, None, :]

    x = jnp.concatenate([x_prompt.reshape(tp, d), x_sample.reshape(ts, d)], axis=0)

    conv_state = jnp.zeros((nseq, CONV_PAD, conv_ch), F32).at[nbp:, CONV_PAD - CONV_BUF:].set(state_conv)
    pool_state = jnp.zeros((nseq, POOL_PAD, pool_ch), F32).at[nbp:, POOL_PAD - POOL_BUF:].set(state_pool)

    depth = norm_g.shape[0]
    conv_new = pool_new = k_p = k_s = v_p = v_s = None
    for layer in range(depth):
        modc = modc_all[layer]
        h = _norm_mod(x, norm_g[layer, 0], modc, sc_col=1, sh_col=0)
        if layer % 2 == 0:
            u = _matmul(h, w_in_ab, b_in_ab)
            a_out, conv_new = _conv_branch(u, conv_state, conv_w, conv_b, ln_g, ln_b,
                                           seq_of_chunk, chunk_start)
            b_out, pool_new = _pool_branch(u, (2 * conv_ch) // pool_ch, pool_state, pool_w, pool_b,
                                           pool_scale, seq_of_chunk, chunk_start, chunk_pos)
            y = _matmul(jnp.concatenate([a_out, b_out], axis=1), w_out_ab, None)
        else:
            q = _matmul(h, w_qkv, None, col0=0, ncols=d)
            k_p = _matmul(h, w_qkv, None, col0=d, ncols=d, row0=0, nrows=tp)
            k_s = _matmul(h, w_qkv, None, col0=d, ncols=d, row0=tp, nrows=ts)
            v_p = _matmul(h, w_qkv, None, col0=2 * d, ncols=d, row0=0, nrows=tp)
            v_s = _matmul(h, w_qkv, None, col0=2 * d, ncols=d, row0=tp, nrows=ts)
            o_p = _attn_prompt(q, k_p, v_p, nbp, lp)
            o_s = _attn_sample(q, k_s, v_s, cache_k, cache_v, tp // ls, ls)
            y = _matmul(jnp.concatenate([o_p, o_s], axis=0), w_o, None)
        x1, hf, route, gates_pad, counts = _resid_router(
            x, y, norm_g[layer, 1], norm_g[layer, 2], modc, 2, 4, 3, w_router[layer], b_router[layer])
        counts = counts[0, :w_router.shape[2]].astype(jnp.int32)
        x = _moe_layer(x1, hf, route, counts, gates_pad, layer, w_up, b_up, w_down, b_down,
                       norm_g[layer, 3], modc, 5)

    y_prompt = x[:tp].reshape(nbp, lp, d)
    y_sample = x[tp:].reshape(nbs, ls, d)
    conv_new = conv_new[:, CONV_PAD - CONV_BUF:]
    pool_new = pool_new[:, POOL_PAD - POOL_BUF:]
    return (y_prompt, y_sample,
            conv_new[:nbp], conv_new[nbp:], pool_new[:nbp], pool_new[nbp:],
            k_p.reshape(nbp, lp, nh, HEAD_DIM), k_s.reshape(nbs, ls, nh, HEAD_DIM),
            v_p.reshape(nbp, lp, nh, HEAD_DIM), v_s.reshape(nbs, ls, nh, HEAD_DIM))
```

```python
import functools
import math

import numpy as np
import jax
import jax.numpy as jnp
from jax import lax
from jax.experimental import pallas as pl
from jax.experimental.pallas import tpu as pltpu

F32 = jnp.float32
BF16 = jnp.bfloat16

LANES = 128
VMEM_LIMIT = 56 * 1024 * 1024

RMS_EPS = 1e-6
LN_EPS = 1e-5
CONV_WIDTH = 31
CONV_BUF = CONV_WIDTH - 1
POOL_WINDOWS = (2, 4, 8, 16)
POOL_BUF = max(POOL_WINDOWS) - 1
HEAD_DIM = 128
TOP_K = 4
SWIGLU_LIMIT = 7.0
SWIGLU_ALPHA = 1.702
SB_SCALE = 1.0 / math.sqrt(HEAD_DIM)
UNDERFLOW = -104.0

CHUNK = 64
ROW_TILE = 256
MOE_TM = 256
COMBINE_TT = 128
CONV_PAD = 32
POOL_PAD = 16
SB_BLOCK = 128
SB_WINDOW_BLOCKS = 3
SB_TILES_PER_STEP = 4
SAMPLE_TKC = 256
BIG_POS = 2 ** 30
SLAB_PAD = 8


def _cparams(sem, vmem=VMEM_LIMIT):
    return pltpu.CompilerParams(dimension_semantics=sem, vmem_limit_bytes=vmem)


def _rms(x):
    return x * lax.rsqrt(jnp.mean(x * x, axis=-1, keepdims=True) + RMS_EPS)


def _mod_kernel(c_ref, w_ref, b_ref, o_ref):
    c = c_ref[...]
    s = (c * jax.nn.sigmoid(c)).astype(BF16)
    o_ref[...] = jnp.dot(s, w_ref[...].astype(BF16), preferred_element_type=F32) + b_ref[...]


def _modulation(c_all, w_mod, b_mod, tn=512):
    depth, d, n = w_mod.shape
    ns = c_all.shape[0]
    return pl.pallas_call(
        _mod_kernel,
        out_shape=jax.ShapeDtypeStruct((depth, ns, n), F32),
        grid=(depth, n // tn),
        in_specs=[
            pl.BlockSpec((ns, d), lambda l, j: (0, 0)),
            pl.BlockSpec((None, d, tn), lambda l, j: (l, 0, j)),
            pl.BlockSpec((None, 1, tn), lambda l, j: (l, 0, j)),
        ],
        out_specs=pl.BlockSpec((None, ns, tn), lambda l, j: (l, 0, j)),
        compiler_params=_cparams(("arbitrary", "arbitrary")),
        name="modulation",
    )(c_all, w_mod, b_mod.reshape(depth, 1, n))


def _norm_mod_kernel(x_ref, g_ref, sc_ref, sh_ref, o_ref, *, nch):
    for c in range(nch):
        rows = slice(c * CHUNK, (c + 1) * CHUNK)
        y = _rms(x_ref[rows, :]) * g_ref[...]
        o_ref[rows, :] = (y * (1.0 + sc_ref[c]) + sh_ref[c]).astype(o_ref.dtype)


def _norm_mod(x, g, modc, sc_col, sh_col):
    t, d = x.shape
    nch = ROW_TILE // CHUNK
    mspec = lambda col: pl.BlockSpec((nch, 1, d), lambda i: (i, 0, col))
    return pl.pallas_call(
        functools.partial(_norm_mod_kernel, nch=nch),
        out_shape=jax.ShapeDtypeStruct((t, d), BF16),
        grid=(t // ROW_TILE,),
        in_specs=[
            pl.BlockSpec((ROW_TILE, d), lambda i: (i, 0)),
            pl.BlockSpec((1, d), lambda i: (0, 0)),
            mspec(sc_col), mspec(sh_col),
        ],
        out_specs=pl.BlockSpec((ROW_TILE, d), lambda i: (i, 0)),
        compiler_params=_cparams(("arbitrary",)),
        name="norm_mod",
    )(x, g.reshape(1, d), modc, modc)


def _mm_kernel(a_ref, w_ref, b_ref, o_ref, w_bf):
    @pl.when(pl.program_id(1) == 0)
    def _():
        w_bf[...] = w_ref[...].astype(BF16)

    o_ref[...] = (jnp.dot(a_ref[...], w_bf[...], preferred_element_type=F32)
                  + b_ref[...]).astype(o_ref.dtype)


def _matmul(a, w, bias, *, col0=0, ncols=None, row0=0, nrows=None, out_dtype=F32, tm=1024, tn=512):
    kdim = a.shape[1]
    m = a.shape[0] if nrows is None else nrows
    n_total = w.shape[1]
    ncols = n_total if ncols is None else ncols
    tm = math.gcd(tm, m)
    tn = math.gcd(tn, ncols)
    assert a.dtype == BF16 and tm % 16 == 0 and tn % LANES == 0 and col0 % tn == 0 and row0 % tm == 0
    joff = col0 // tn
    ioff = row0 // tm
    if bias is None:
        bias = jnp.zeros((n_total,), F32)
    return pl.pallas_call(
        _mm_kernel,
        out_shape=jax.ShapeDtypeStruct((m, ncols), out_dtype),
        grid=(ncols // tn, m // tm),
        in_specs=[
            pl.BlockSpec((tm, kdim), lambda j, i: (i + ioff, 0)),
            pl.BlockSpec((kdim, tn), lambda j, i: (0, j + joff)),
            pl.BlockSpec((1, tn), lambda j, i: (0, j + joff)),
        ],
        out_specs=pl.BlockSpec((tm, tn), lambda j, i: (i, j)),
        scratch_shapes=[pltpu.VMEM((kdim, tn), BF16)],
        compiler_params=_cparams(("arbitrary", "arbitrary")),
        name="matmul",
    )(a, w, bias.reshape(1, n_total))


def _conv_kernel(seq_ref, start_ref, av_ref, ag_ref, st_ref, w_ref, cb_ref, lg_ref, lb_ref,
                 o_ref, sto_ref, gbuf, cbuf, *, cblk):
    i = pl.program_id(0)
    ch = av_ref.shape[1]

    @pl.when(start_ref[i] == 1)
    def _():
        gbuf[0:CONV_PAD, :] = st_ref[...]

    @pl.when(start_ref[i] == 0)
    def _():
        gbuf[0:CONV_PAD, :] = gbuf[CHUNK:CHUNK + CONV_PAD, :]

    gbuf[CONV_PAD:CONV_PAD + CHUNK, :] = av_ref[...] * jax.nn.sigmoid(ag_ref[...])

    off = CONV_PAD - CONV_BUF
    for c0 in range(0, ch, cblk):
        cols = slice(c0, c0 + cblk)
        acc = jnp.zeros((CHUNK, cblk), F32)
        for j in range(CONV_WIDTH):
            acc = acc + w_ref[j:j + 1, cols] * gbuf[off + j:off + j + CHUNK, cols]
        cbuf[:, cols] = acc + cb_ref[:, cols]

    conv = cbuf[...]
    mu = jnp.mean(conv, axis=-1, keepdims=True)
    cen = conv - mu
    var = jnp.mean(cen * cen, axis=-1, keepdims=True)
    y = cen * lax.rsqrt(var + LN_EPS) * lg_ref[...] + lb_ref[...]
    o_ref[...] = (y * jax.nn.sigmoid(y)).astype(o_ref.dtype)
    sto_ref[...] = gbuf[CHUNK:CHUNK + CONV_PAD, :]


def _conv_branch(u, state_pad, conv_w, conv_b, ln_g, ln_b, seq_of_chunk, chunk_start):
    t = u.shape[0]
    ch = conv_w.shape[1]
    nseq = state_pad.shape[0]
    nchunks = t // CHUNK
    grid_spec = pltpu.PrefetchScalarGridSpec(
        num_scalar_prefetch=2,
        grid=(nchunks,),
        in_specs=[
            pl.BlockSpec((CHUNK, ch), lambda i, s, f: (i, 0)),
            pl.BlockSpec((CHUNK, ch), lambda i, s, f: (i, 1)),
            pl.BlockSpec((None, CONV_PAD, ch), lambda i, s, f: (s[i], 0, 0)),
            pl.BlockSpec((CONV_WIDTH, ch), lambda i, s, f: (0, 0)),
            pl.BlockSpec((1, ch), lambda i, s, f: (0, 0)),
            pl.BlockSpec((1, ch), lambda i, s, f: (0, 0)),
            pl.BlockSpec((1, ch), lambda i, s, f: (0, 0)),
        ],
        out_specs=[
            pl.BlockSpec((CHUNK, ch), lambda i, s, f: (i, 0)),
            pl.BlockSpec((None, CONV_PAD, ch), lambda i, s, f: (s[i], 0, 0)),
        ],
        scratch_shapes=[pltpu.VMEM((CONV_PAD + CHUNK, ch), F32), pltpu.VMEM((CHUNK, ch), F32)],
    )
    return pl.pallas_call(
        functools.partial(_conv_kernel, cblk=min(256, ch)),
        out_shape=[jax.ShapeDtypeStruct((t, ch), BF16),
                   jax.ShapeDtypeStruct((nseq, CONV_PAD, ch), F32)],
        grid_spec=grid_spec,
        compiler_params=_cparams(("arbitrary",)),
        name="conv_branch",
    )(seq_of_chunk, chunk_start, u, u, state_pad, conv_w, conv_b.reshape(1, ch),
      ln_g.reshape(1, ch), ln_b.reshape(1, ch))


def _pool_kernel(seq_ref, start_ref, pos_ref, p_ref, st_ref, w_ref, b_ref, sc_ref,
                 o_ref, sto_ref, pbuf):
    i = pl.program_id(0)
    gch = w_ref.shape[1]

    @pl.when(start_ref[i] == 1)
    def _():
        pbuf[0:POOL_PAD, :] = st_ref[...]

    @pl.when(start_ref[i] == 0)
    def _():
        pbuf[0:POOL_PAD, :] = pbuf[CHUNK:CHUNK + POOL_PAD, :]

    pbuf[POOL_PAD:POOL_PAD + CHUNK, :] = p_ref[...]

    pos = pos_ref[i] + lax.broadcasted_iota(jnp.int32, (CHUNK, 1), 0)
    for gi, win in enumerate(POOL_WINDOWS):
        cols = slice(gi * gch, (gi + 1) * gch)
        cur = pbuf[POOL_PAD:POOL_PAD + CHUNK, cols]
        wsum = cur
        for j in range(1, win):
            wsum = wsum + pbuf[POOL_PAD - j:POOL_PAD - j + CHUNK, cols]
        count = jnp.minimum(pos + 1, win).astype(F32)
        diff = wsum / count - cur
        y = jnp.dot(diff.astype(BF16), w_ref[gi], preferred_element_type=F32) + b_ref[gi]
        o_ref[:, cols] = (y * sc_ref[:, cols]).astype(o_ref.dtype)
    sto_ref[...] = pbuf[CHUNK:CHUNK + POOL_PAD, :]


def _pool_branch(u, col_block, state_pad, pool_w, pool_b, pool_scale,
                 seq_of_chunk, chunk_start, chunk_pos):
    t = u.shape[0]
    ng, gch, _ = pool_w.shape
    ch = ng * gch
    nseq = state_pad.shape[0]
    grid_spec = pltpu.PrefetchScalarGridSpec(
        num_scalar_prefetch=3,
        grid=(t // CHUNK,),
        in_specs=[
            pl.BlockSpec((CHUNK, ch), lambda i, s, f, p: (i, col_block)),
            pl.BlockSpec((None, POOL_PAD, ch), lambda i, s, f, p: (s[i], 0, 0)),
            pl.BlockSpec((ng, gch, gch), lambda i, s, f, p: (0, 0, 0)),
            pl.BlockSpec((ng, 1, gch), lambda i, s, f, p: (0, 0, 0)),
            pl.BlockSpec((1, ch), lambda i, s, f, p: (0, 0)),
        ],
        out_specs=[
            pl.BlockSpec((CHUNK, ch), lambda i, s, f, p: (i, 0)),
            pl.BlockSpec((None, POOL_PAD, ch), lambda i, s, f, p: (s[i], 0, 0)),
        ],
        scratch_shapes=[pltpu.VMEM((POOL_PAD + CHUNK, ch), F32)],
    )
    return pl.pallas_call(
        _pool_kernel,
        out_shape=[jax.ShapeDtypeStruct((t, ch), BF16),
                   jax.ShapeDtypeStruct((nseq, POOL_PAD, ch), F32)],
        grid_spec=grid_spec,
        compiler_params=_cparams(("arbitrary",)),
        name="pool_branch",
    )(seq_of_chunk, chunk_start, chunk_pos, u, state_pad, pool_w.astype(BF16),
      pool_b.reshape(ng, 1, gch), pool_scale.reshape(1, ch))


def _resid_router_kernel(x_ref, y_ref, g1_ref, g2_ref, gt_ref, sc_ref, sh_ref, wr_ref, br_ref,
                         x1_ref, h3_ref, idx_ref, gate_ref, cnt_ref, h_ref, *, nch):
    nsub = h3_ref.shape[1]
    for c in range(nch):
        rows = slice(c * CHUNK, (c + 1) * CHUNK)
        x1 = x_ref[rows, :] + gt_ref[c] * (_rms(y_ref[rows, :]) * g1_ref[...])
        x1_ref[rows, :] = x1
        h_ref[rows, :] = (_rms(x1) * g2_ref[...]) * (1.0 + sc_ref[c]) + sh_ref[c]

    for kk in range(nsub):
        h3_ref[:, kk, :] = h_ref[:, kk * LANES:(kk + 1) * LANES]

    logits = jnp.dot(h_ref[...].astype(BF16), wr_ref[...], preferred_element_type=F32) + br_ref[...]
    lane = lax.broadcasted_iota(jnp.int32, logits.shape, 1).astype(F32)
    vals, idxs = [], []
    for _ in range(TOP_K):
        m = jnp.max(logits, axis=-1, keepdims=True)
        sel = jnp.min(jnp.where(logits == m, lane, float(LANES)), axis=-1, keepdims=True)
        vals.append(m)
        idxs.append(sel)
        logits = jnp.where(lane == sel, -jnp.inf, logits)
    exps = [jnp.exp(v - vals[0]) for v in vals]
    den = exps[0] + exps[1] + exps[2] + exps[3]

    @pl.when(pl.program_id(0) == 0)
    def _():
        cnt_ref[...] = jnp.zeros_like(cnt_ref)

    tr = logits.shape[0]
    onehot = jnp.zeros(logits.shape, F32)
    for k in range(TOP_K):
        onehot = onehot + jnp.where(lane == idxs[k], 1.0, 0.0)
    rr = lax.broadcasted_iota(jnp.int32, (tr, tr), 0)
    cc = lax.broadcasted_iota(jnp.int32, (tr, tr), 1)
    lower = jnp.where(cc < rr, 1.0, 0.0).astype(BF16)
    before = jnp.dot(lower, onehot.astype(BF16), preferred_element_type=F32) + cnt_ref[...]
    cnt_ref[...] += jnp.sum(onehot, axis=0, keepdims=True)

    gate_out = jnp.zeros(logits.shape, F32)
    idx_out = jnp.zeros(logits.shape, F32)
    for k in range(TOP_K):
        rank = jnp.sum(jnp.where(lane == idxs[k], before, 0.0), axis=-1, keepdims=True)
        gate_out = jnp.where(lane == float(k), exps[k] / den, gate_out)
        idx_out = jnp.where(lane == float(k), idxs[k], idx_out)
        idx_out = jnp.where(lane == float(TOP_K + k), rank, idx_out)
    gate_ref[...] = gate_out
    idx_ref[...] = idx_out.astype(jnp.int32)


def _resid_router(x, y, g1, g2, modc, gt_col, sc_col, sh_col, w_router, b_router):
    t, d = x.shape
    ne = w_router.shape[1]
    nch = ROW_TILE // CHUNK
    wr = jnp.zeros((d, LANES), BF16).at[:, :ne].set(w_router.astype(BF16))
    br = jnp.full((1, LANES), -1e30, F32).at[0, :ne].set(b_router.astype(F32))
    mspec = lambda col: pl.BlockSpec((nch, 1, d), lambda i: (i, 0, col))
    row = pl.BlockSpec((ROW_TILE, d), lambda i: (i, 0))
    vec = pl.BlockSpec((1, d), lambda i: (0, 0))
    small = pl.BlockSpec((ROW_TILE, LANES), lambda i: (i, 0))
    return pl.pallas_call(
        functools.partial(_resid_router_kernel, nch=nch),
        out_shape=[jax.ShapeDtypeStruct((t, d), F32),
                   jax.ShapeDtypeStruct((t, d // LANES, LANES), F32),
                   jax.ShapeDtypeStruct((t, LANES), jnp.int32),
                   jax.ShapeDtypeStruct((t, LANES), F32),
                   jax.ShapeDtypeStruct((1, LANES), F32)],
        grid=(t // ROW_TILE,),
        in_specs=[row, row, vec, vec, mspec(gt_col), mspec(sc_col), mspec(sh_col),
                  pl.BlockSpec((d, LANES), lambda i: (0, 0)),
                  pl.BlockSpec((1, LANES), lambda i: (0, 0))],
        out_specs=[row, pl.BlockSpec((ROW_TILE, d // LANES, LANES), lambda i: (i, 0, 0)),
                   small, small, pl.BlockSpec((1, LANES), lambda i: (0, 0))],
        scratch_shapes=[pltpu.VMEM((ROW_TILE, d), F32)],
        compiler_params=_cparams(("arbitrary",)),
        name="resid_router",
    )(x, y, g1.reshape(1, d), g2.reshape(1, d), modc, modc, modc, wr, br)


def _moe_gather_kernel(tok_ref, nv_ref, h3_hbm, o_ref, buf, tmp, sem, *, tm, nsub):
    i = pl.program_id(0)
    nv = nv_ref[0]

    def issue(tile, slot):
        def body(r, carry):
            tok = tok_ref[tile * tm + r]
            pltpu.make_async_copy(h3_hbm.at[tok], buf.at[slot * tm + r, pl.ds(0, nsub)],
                                  sem.at[slot]).start()
            return carry
        lax.fori_loop(0, tm, body, 0)

    @pl.when(i == 0)
    def _():
        issue(0, 0)

    @pl.when(i + 1 < nv)
    def _():
        issue(i + 1, (i + 1) % 2)

    @pl.when(i < nv)
    def _():
        slot = i % 2
        base = pl.multiple_of(slot * tm, tm)
        pltpu.make_async_copy(h3_hbm.at[pl.ds(0, tm)], buf.at[pl.ds(base, tm), pl.ds(0, nsub)],
                              sem.at[slot]).wait()
        for kk in range(nsub):
            tmp[:, kk * LANES:(kk + 1) * LANES] = buf[pl.ds(base, tm), kk, :]
        o_ref[...] = tmp[...].astype(o_ref.dtype)

    @pl.when(i >= nv)
    def _():
        o_ref[...] = jnp.zeros_like(o_ref)


def _moe_gather(slot_token, n_valid, h3, n_tiles):
    t, nsub, _ = h3.shape
    tm = MOE_TM
    grid_spec = pltpu.PrefetchScalarGridSpec(
        num_scalar_prefetch=2,
        grid=(n_tiles,),
        in_specs=[pl.BlockSpec(memory_space=pl.ANY)],
        out_specs=pl.BlockSpec((tm, nsub * LANES), lambda i, tok, nv: (i, 0)),
        scratch_shapes=[pltpu.VMEM((2 * tm, nsub + SLAB_PAD, LANES), F32),
                        pltpu.VMEM((tm, nsub * LANES), F32), pltpu.SemaphoreType.DMA((2,))],
    )
    return pl.pallas_call(
        functools.partial(_moe_gather_kernel, tm=tm, nsub=nsub),
        out_shape=jax.ShapeDtypeStruct((n_tiles * tm, nsub * LANES), BF16),
        grid_spec=grid_spec,
        compiler_params=_cparams(("arbitrary",)),
        name="moe_gather",
    )(slot_token, n_valid, h3)


def _cast_rows(src, dst, rows=256):
    n = src.shape[0]
    for r in range(0, n, rows):
        dst[r:r + rows, :] = src[r:r + rows, :].astype(dst.dtype)


def _moe_up_kernel(te_ref, first_ref, nxt_ref, rows_ref, x_ref, w_hbm, bg_ref, bl_ref, o_ref,
                   stage, w_bf, sem, *, layer, tn, nj):
    j = pl.program_id(0)
    i = pl.program_id(1)

    def copies(e, jj):
        glu = pl.ds(pl.multiple_of(jj * tn, tn), tn)
        lin = pl.ds(pl.multiple_of((nj + jj) * tn, tn), tn)
        return (pltpu.make_async_copy(w_hbm.at[layer, e, :, glu], stage.at[0], sem.at[0]),
                pltpu.make_async_copy(w_hbm.at[layer, e, :, lin], stage.at[1], sem.at[1]))

    def start(e, jj):
        for c in copies(e, jj):
            c.start()

    @pl.when(jnp.logical_and(i == 0, j == 0))
    def _():
        start(te_ref[0], 0)

    @pl.when(first_ref[i] == 1)
    def _():
        for c in copies(te_ref[i], j):
            c.wait()
        _cast_rows(stage.at[0], w_bf.at[0])
        _cast_rows(stage.at[1], w_bf.at[1])
        nxt = nxt_ref[i]

        @pl.when(nxt >= 0)
        def _():
            start(nxt, j)

        @pl.when(jnp.logical_and(nxt < 0, j + 1 < nj))
        def _():
            start(te_ref[0], j + 1)

    def compute(rows):
        x = x_ref[rows, :]
        hg = jnp.dot(x, w_bf[0], preferred_element_type=F32) + bg_ref[...]
        hl = jnp.dot(x, w_bf[1], preferred_element_type=F32) + bl_ref[...]
        x_glu = jnp.minimum(hg, SWIGLU_LIMIT)
        x_lin = jnp.clip(hl, -SWIGLU_LIMIT, SWIGLU_LIMIT)
        act = x_glu * jax.nn.sigmoid(SWIGLU_ALPHA * x_glu) * (x_lin + 1.0)
        o_ref[rows, :] = act.astype(o_ref.dtype)

    tm = x_ref.shape[0]
    nrows = rows_ref[i]

    @pl.when(nrows > tm // 2)
    def _():
        compute(slice(0, tm))

    @pl.when(nrows <= tm // 2)
    def _():
        o_ref[tm // 2:tm, :] = jnp.zeros((tm - tm // 2, o_ref.shape[1]), o_ref.dtype)

    @pl.when(jnp.logical_and(nrows > 0, nrows <= tm // 2))
    def _():
        compute(slice(0, tm // 2))

    @pl.when(nrows == 0)
    def _():
        o_ref[0:tm // 2, :] = jnp.zeros((tm // 2, o_ref.shape[1]), o_ref.dtype)


def _moe_up(x_sorted, w_up, b_up, layer, tile_expert, tile_first, tile_next, tile_rows, tn=512):
    n_pad, d = x_sorted.shape
    _, ne, _, f2 = w_up.shape
    f = f2 // 2
    tm = MOE_TM
    n_tiles = n_pad // tm
    tn = min(tn, f)
    nj = f // tn
    grid_spec = pltpu.PrefetchScalarGridSpec(
        num_scalar_prefetch=4,
        grid=(nj, n_tiles),
        in_specs=[
            pl.BlockSpec((tm, d), lambda j, i, te, fi, nx, nv: (i, 0)),
            pl.BlockSpec(memory_space=pl.ANY),
            pl.BlockSpec((None, None, 1, tn), lambda j, i, te, fi, nx, nv: (layer, te[i], 0, j)),
            pl.BlockSpec((None, None, 1, tn), lambda j, i, te, fi, nx, nv: (layer, te[i], 0, nj + j)),
        ],
        out_specs=pl.BlockSpec((tm, tn), lambda j, i, te, fi, nx, nv: (i, j)),
        scratch_shapes=[pltpu.VMEM((2, d, tn), F32), pltpu.VMEM((2, d, tn), BF16),
                        pltpu.SemaphoreType.DMA((2,))],
    )
    b4 = b_up.reshape(b_up.shape[0], ne, 1, f2)
    return pl.pallas_call(
        functools.partial(_moe_up_kernel, layer=layer, tn=tn, nj=nj),
        out_shape=jax.ShapeDtypeStruct((n_pad, f), BF16),
        grid_spec=grid_spec,
        compiler_params=_cparams(("arbitrary", "arbitrary")),
        name="moe_up",
    )(tile_expert, tile_first, tile_next, tile_rows, x_sorted, w_up, b4, b4)


def _moe_down_kernel(te_ref, first_ref, nxt_ref, rows_ref, a_ref, w_hbm, b_ref, o_ref,
                     stage, w_bf, sem, *, layer, tn, nj):
    j = pl.program_id(0)
    i = pl.program_id(1)

    def copy(e, jj):
        cols = pl.ds(pl.multiple_of(jj * tn, tn), tn)
        return pltpu.make_async_copy(w_hbm.at[layer, e, :, cols], stage, sem.at[0])

    @pl.when(jnp.logical_and(i == 0, j == 0))
    def _():
        copy(te_ref[0], 0).start()

    @pl.when(first_ref[i] == 1)
    def _():
        copy(te_ref[i], j).wait()
        _cast_rows(stage, w_bf)
        nxt = nxt_ref[i]

        @pl.when(nxt >= 0)
        def _():
            copy(nxt, j).start()

        @pl.when(jnp.logical_and(nxt < 0, j + 1 < nj))
        def _():
            copy(te_ref[0], j + 1).start()

    nsub = o_ref.shape[1]

    def compute(rows):
        res = jnp.dot(a_ref[rows, :], w_bf[...], preferred_element_type=F32) + b_ref[...]
        for c in range(nsub):
            o_ref[rows, c, :] = res[:, c * LANES:(c + 1) * LANES]

    tm = a_ref.shape[0]
    nrows = rows_ref[i]

    @pl.when(nrows > tm // 2)
    def _():
        compute(slice(0, tm))

    @pl.when(nrows <= tm // 2)
    def _():
        o_ref[tm // 2:tm] = jnp.zeros((tm - tm // 2, nsub, LANES), o_ref.dtype)

    @pl.when(jnp.logical_and(nrows > 0, nrows <= tm // 2))
    def _():
        compute(slice(0, tm // 2))

    @pl.when(nrows == 0)
    def _():
        o_ref[0:tm // 2] = jnp.zeros((tm // 2, nsub, LANES), o_ref.dtype)


def _moe_down(act, w_down, b_down, layer, tile_expert, tile_first, tile_next, tile_rows, tn=2048):
    n_pad, f = act.shape
    _, ne, _, d = w_down.shape
    tm = MOE_TM
    n_tiles = n_pad // tm
    tn = min(tn, d)
    nj = d // tn
    nsub = tn // LANES
    grid_spec = pltpu.PrefetchScalarGridSpec(
        num_scalar_prefetch=4,
        grid=(nj, n_tiles),
        in_specs=[
            pl.BlockSpec((tm, f), lambda j, i, te, fi, nx, nv: (i, 0)),
            pl.BlockSpec(memory_space=pl.ANY),
            pl.BlockSpec((None, None, 1, tn), lambda j, i, te, fi, nx, nv: (layer, te[i], 0, j)),
        ],
        out_specs=pl.BlockSpec((tm, nsub, LANES), lambda j, i, te, fi, nx, nv: (i, j, 0)),
        scratch_shapes=[pltpu.VMEM((f, tn), F32), pltpu.VMEM((f, tn), BF16),
                        pltpu.SemaphoreType.DMA((1,))],
    )
    return pl.pallas_call(
        functools.partial(_moe_down_kernel, layer=layer, tn=tn, nj=nj),
        out_shape=jax.ShapeDtypeStruct((n_pad, d // LANES, LANES), F32),
        grid_spec=grid_spec,
        compiler_params=_cparams(("arbitrary", "arbitrary")),
        name="moe_down",
    )(tile_expert, tile_first, tile_next, tile_rows, act, w_down,
      b_down.reshape(b_down.shape[0], ne, 1, d))


def _moe_combine_kernel(slot_ref, y3_hbm, x_ref, gate_ref, g_ref, gt_ref, *refs,
                        tt, nt, nsub, npt):
    out_refs, (buf, sum3, ybuf, sem) = refs[:-4], refs[-4:]
    i = pl.program_id(0)
    rows_per_slot = TOP_K * tt

    def issue(tile, s):
        def body(r, carry):
            for k in range(TOP_K):
                src = slot_ref[(tile * tt + r) * TOP_K + k]
                pltpu.make_async_copy(y3_hbm.at[src],
                                      buf.at[s * rows_per_slot + k * tt + r, pl.ds(0, nsub)],
                                      sem.at[s]).start()
            return carry
        lax.fori_loop(0, tt, body, 0)

    @pl.when(i == 0)
    def _():
        issue(0, 0)

    @pl.when(i + 1 < nt)
    def _():
        issue(i + 1, (i + 1) % 2)

    s = i % 2
    base = pl.multiple_of(s * rows_per_slot, rows_per_slot)
    pltpu.make_async_copy(y3_hbm.at[pl.ds(0, rows_per_slot)],
                          buf.at[pl.ds(base, rows_per_slot), pl.ds(0, nsub)], sem.at[s]).wait()

    def token_body(r, carry):
        acc = gate_ref[r, 0:1, :] * buf[base + r, pl.ds(0, nsub), :]
        for k in range(1, TOP_K):
            acc = acc + gate_ref[r, k:k + 1, :] * buf[base + k * tt + r, pl.ds(0, nsub), :]
        sum3[r, pl.ds(0, nsub), :] = acc
        return carry
    lax.fori_loop(0, tt, token_body, 0)

    for kk in range(nsub):
        ybuf[:, kk * LANES:(kk + 1) * LANES] = sum3[:, kk, :]
    def write(o_ref):
        for c in range(tt // CHUNK):
            rows = slice(c * CHUNK, (c + 1) * CHUNK)
            o_ref[rows, :] = x_ref[rows, :] + gt_ref[c] * (_rms(ybuf[rows, :]) * g_ref[...])

    if npt is None:
        write(out_refs[0])
    else:
        pl.when(i < npt)(lambda: write(out_refs[0]))
        pl.when(i >= npt)(lambda: write(out_refs[1]))


def _moe_combine(slot, y3, x1, gates_pad, g, modc, gt_col, split=None):
    t, d = x1.shape
    nsub = d // LANES
    tt = COMBINE_TT
    nt = t // tt
    nch = tt // CHUNK
    gates_rep = jnp.broadcast_to(gates_pad[:, :TOP_K, None], (t, TOP_K, LANES))
    if split is None:
        npt = None
        out_shape = jax.ShapeDtypeStruct((t, d), F32)
        out_specs = pl.BlockSpec((tt, d), lambda i, sl: (i, 0))
    else:
        assert split % tt == 0 and 0 < split < t
        npt = split // tt
        out_shape = [jax.ShapeDtypeStruct((split, d), F32), jax.ShapeDtypeStruct((t - split, d), F32)]
        out_specs = [pl.BlockSpec((tt, d), lambda i, sl: (jnp.minimum(i, npt - 1), 0)),
                     pl.BlockSpec((tt, d), lambda i, sl: (jnp.maximum(i - npt, 0), 0))]
    grid_spec = pltpu.PrefetchScalarGridSpec(
        num_scalar_prefetch=1,
        grid=(nt,),
        in_specs=[
            pl.BlockSpec(memory_space=pl.ANY),
            pl.BlockSpec((tt, d), lambda i, sl: (i, 0)),
            pl.BlockSpec((tt, TOP_K, LANES), lambda i, sl: (i, 0, 0)),
            pl.BlockSpec((1, d), lambda i, sl: (0, 0)),
            pl.BlockSpec((nch, 1, d), lambda i, sl: (i, 0, gt_col)),
        ],
        out_specs=out_specs,
        scratch_shapes=[pltpu.VMEM((2 * TOP_K * tt, nsub + SLAB_PAD, LANES), F32),
                        pltpu.VMEM((tt, nsub + SLAB_PAD, LANES), F32),
                        pltpu.VMEM((tt, d), F32), pltpu.SemaphoreType.DMA((2,))],
    )
    return pl.pallas_call(
        functools.partial(_moe_combine_kernel, tt=tt, nt=nt, nsub=nsub, npt=npt),
        out_shape=out_shape,
        grid_spec=grid_spec,
        compiler_params=_cparams(("arbitrary",)),
        name="moe_combine",
    )(slot, y3, x1, gates_rep, g.reshape(1, d), modc)


def _moe_layer(x1, h3, route, counts, gates_pad, layer, w_up, b_up, w_down, b_down, g_post, modc,
               gt_col, split):
    t = h3.shape[0]
    ne = w_up.shape[1]
    tm = MOE_TM
    n_assign = t * TOP_K
    n_tiles = n_assign // tm + ne

    flat_e = route[:, :TOP_K].reshape(-1)
    rank = route[:, TOP_K:2 * TOP_K].reshape(-1)
    padded = (counts + tm - 1) // tm * tm
    pad_end = jnp.cumsum(padded)
    pad_start = pad_end - padded
    slot = (pad_start[flat_e] + rank).astype(jnp.int32)
    slot_token = jnp.zeros((n_tiles * tm,), jnp.int32).at[slot].set(
        jnp.arange(n_assign, dtype=jnp.int32) // TOP_K)
    n_valid = (pad_end[-1] // tm).astype(jnp.int32)
    tile_ids = jnp.arange(n_tiles, dtype=jnp.int32)
    tile_start = jnp.minimum(tile_ids, n_valid - 1) * tm
    tile_expert = jnp.sum((pad_end[None, :] <= tile_start[:, None]).astype(jnp.int32), axis=1)
    tile_expert = jnp.minimum(tile_expert, ne - 1)
    tile_rows = jnp.clip(counts[tile_expert] - (tile_start - pad_start[tile_expert]), 0, tm)
    tile_rows = jnp.where(tile_ids < n_valid, tile_rows, 0).astype(jnp.int32)
    tile_first = jnp.concatenate([jnp.ones((1,), jnp.int32),
                                  (tile_expert[1:] != tile_expert[:-1]).astype(jnp.int32)])
    first_pos = jnp.where(tile_first == 1, tile_ids, n_tiles)
    next_pos = jnp.concatenate([lax.cummin(first_pos, reverse=True)[1:],
                                jnp.full((1,), n_tiles, jnp.int32)])
    tile_next = jnp.where(next_pos < n_tiles, tile_expert[jnp.minimum(next_pos, n_tiles - 1)], -1)
    tile_next = tile_next.astype(jnp.int32)

    nv = n_valid.reshape(1)
    x_sorted = _moe_gather(slot_token, nv, h3, n_tiles)
    act = _moe_up(x_sorted, w_up, b_up, layer, tile_expert, tile_first, tile_next, tile_rows)
    y3 = _moe_down(act, w_down, b_down, layer, tile_expert, tile_first, tile_next, tile_rows)
    return _moe_combine(slot, y3, x1, gates_pad, g_post, modc, gt_col, split)


def _log_gates(z):
    sp = jnp.log(1.0 + jnp.exp(-jnp.abs(z)))
    return jnp.minimum(z, 0.0) - sp, jnp.minimum(-z, 0.0) - sp


def _split_bf16(x):
    hi = x.astype(BF16)
    return hi, (x - hi.astype(F32)).astype(BF16)


def _sb_window(q, k_ref, v_ref, k0, nblk, carry, qpos, klim, tri):
    gates = _sb_scores(q, k_ref, k0, nblk, qpos, klim)
    return _sb_weights(gates, _sb_suffix(gates, tri), carry, v_ref, k0, nblk)


def _sb_scores(q, k_ref, k0, nblk, qpos, klim):
    tb = SB_BLOCK
    tq = q.shape[0]
    col = lax.broadcasted_iota(jnp.int32, (tq, tb), 1)
    lim = jnp.minimum(qpos, klim)
    log_beta, log_keep, masks = [], [], []
    for b in range(nblk):
        kt = k_ref[pl.ds(pl.multiple_of(k0 + b * tb, tb), tb), :]
        z = lax.dot_general(q, kt, (((1,), (1,)), ((), ())), preferred_element_type=F32) * SB_SCALE
        lb, lk = _log_gates(z)
        mask = (k0 + b * tb + col) < lim
        log_beta.append(lb)
        log_keep.append(jnp.where(mask, lk, 0.0))
        masks.append(mask)
    return log_beta, log_keep, masks


def _sb_suffix(gates, tri):
    hi, lo = _split_bf16(jnp.concatenate(gates[1], axis=0))
    return jnp.dot(jnp.concatenate([hi, lo], axis=0), tri, preferred_element_type=F32)


def _sb_weights(gates, suf, carry, v_ref, k0, nblk):
    log_beta, log_keep, masks = gates
    tb = SB_BLOCK
    tq = log_beta[0].shape[0]
    a_blocks = [None] * nblk
    for b in reversed(range(nblk)):
        s = suf[b * tq:(b + 1) * tq] + suf[(nblk + b) * tq:(nblk + b + 1) * tq]
        a = jnp.exp(log_beta[b] + (s - log_keep[b]) + carry)
        a_blocks[b] = jnp.where(masks[b], a, 0.0).astype(BF16)
        carry = carry + s[:, 0:1]
    vw = v_ref[pl.ds(pl.multiple_of(k0, tb), nblk * tb), :]
    out = jnp.dot(jnp.concatenate(a_blocks, axis=1), vw, preferred_element_type=F32)
    return out, carry


def _alive(carry):
    return (jnp.max(carry) > UNDERFLOW).astype(jnp.int32)


def _tri(tk):
    r = lax.broadcasted_iota(jnp.int32, (tk, tk), 0)
    c = lax.broadcasted_iota(jnp.int32, (tk, tk), 1)
    return jnp.where(r >= c, 1.0, 0.0).astype(BF16)


def _strict_lower(tq, tk):
    r = lax.broadcasted_iota(jnp.int32, (tq, tk), 0)
    c = lax.broadcasted_iota(jnp.int32, (tq, tk), 1)
    return c < r


def _attn_prompt_kernel(q_ref, k_ref, v_ref, o_ref, kb_ref, vb_ref, *, seq, nblk, ntile):
    tb = SB_BLOCK
    kb_ref[...] = k_ref[...].astype(BF16)
    vb_ref[...] = v_ref[...].astype(BF16)
    tri = _tri(tb)
    row = lax.broadcasted_iota(jnp.int32, (tb, 1), 0)

    def tile_setup(qi):
        r0 = pl.multiple_of(qi * tb, tb)
        q = q_ref[pl.ds(r0, tb), :].astype(BF16)
        return r0, q, r0 + row, jnp.maximum(qi - (nblk - 1), 0)

    def finish(r0, q, qpos, ws, acc, carry):
        def cond(st):
            return jnp.logical_and(st[0] > 0, st[1] > 0)

        def body(st):
            ws, _, acc, carry = st
            ws2 = jnp.maximum(ws - nblk, 0)
            out, carry = _sb_window(q, kb_ref, vb_ref, ws2 * tb, nblk, carry, qpos, ws * tb, tri)
            return ws2, _alive(carry), acc + out, carry

        _, _, acc, _ = lax.while_loop(cond, body, (ws, _alive(carry), acc, carry))
        o_ref[pl.ds(r0, tb), :] = acc.astype(o_ref.dtype)

    def group_body(p, unused):
        tiles = [tile_setup(ntile * p + u) for u in range(ntile)]
        gates = [_sb_scores(q, kb_ref, ws * tb, nblk, qpos, BIG_POS) for _, q, qpos, ws in tiles]
        sufs = [_sb_suffix(g, tri) for g in gates]
        zero = jnp.zeros((tb, 1), F32)
        outs = [_sb_weights(g, s, zero, vb_ref, tl[3] * tb, nblk)
                for g, s, tl in zip(gates, sufs, tiles)]
        for (r0, q, qpos, ws), (acc, carry) in zip(tiles, outs):
            finish(r0, q, qpos, ws, acc, carry)
        return unused

    lax.fori_loop(0, seq // (ntile * tb), group_body, 0)


def _attn_prompt(q, k, v, nb, seq):
    d = q.shape[1]
    nh = d // HEAD_DIM
    nblk = min(SB_WINDOW_BLOCKS, seq // SB_BLOCK)
    ntile = math.gcd(SB_TILES_PER_STEP, seq // SB_BLOCK)
    blk = pl.BlockSpec((seq, HEAD_DIM), lambda b, h: (b, h))
    return pl.pallas_call(
        functools.partial(_attn_prompt_kernel, seq=seq, nblk=nblk, ntile=ntile),
        out_shape=jax.ShapeDtypeStruct((nb * seq, d), BF16),
        grid=(nb, nh),
        in_specs=[blk, blk, blk],
        out_specs=blk,
        scratch_shapes=[pltpu.VMEM((seq, HEAD_DIM), BF16), pltpu.VMEM((seq, HEAD_DIM), BF16)],
        compiler_params=_cparams(("arbitrary", "arbitrary")),
        name="attn_prompt",
    )(q, k, v)


def _attn_sample_kernel(q_ref, kn_ref, vn_ref, ck_hbm, cv_hbm, o_ref,
                        kbuf, vbuf, lb_s, lk_s, hl_s, suf_s, acc_ref, carry_ref, sem,
                        *, nh, lq, tkc, nkt):
    b = pl.program_id(0)
    nrow = nh * lq

    def copies(t, slot):
        k0 = pl.multiple_of((nkt - 1 - t) * tkc, tkc)
        dst = pl.ds(pl.multiple_of(slot * tkc, tkc), tkc)
        return (pltpu.make_async_copy(ck_hbm.at[b, pl.ds(k0, tkc)], kbuf.at[dst], sem.at[0, slot]),
                pltpu.make_async_copy(cv_hbm.at[b, pl.ds(k0, tkc)], vbuf.at[dst], sem.at[1, slot]))

    def start(t, slot):
        for c in copies(t, slot):
            c.start()

    def wait(t, slot):
        for c in copies(t, slot):
            c.wait()

    def heads_tile(get_k, get_v, tk, mask, first):
        tri = _tri(tk)
        for h in range(nh):
            cols = slice(h * HEAD_DIM, (h + 1) * HEAD_DIM)
            rows = slice(h * lq, (h + 1) * lq)
            z = lax.dot_general(q_ref[:, cols].astype(BF16), get_k(h), (((1,), (1,)), ((), ())),
                                preferred_element_type=F32) * SB_SCALE
            lb, lk = _log_gates(z)
            if mask is not None:
                lk = jnp.where(mask, lk, 0.0)
            hi, lo = _split_bf16(lk)
            lb_s[rows, 0:tk] = lb
            lk_s[rows, 0:tk] = lk
            hl_s[rows, 0:tk] = hi
            hl_s[nrow + h * lq:nrow + (h + 1) * lq, 0:tk] = lo
        suf = jnp.dot(hl_s[:, 0:tk], tri, preferred_element_type=F32)
        suf_s[:, 0:tk] = suf[0:nrow] + suf[nrow:2 * nrow]
        for h in range(nh):
            cols = slice(h * HEAD_DIM, (h + 1) * HEAD_DIM)
            rows = slice(h * lq, (h + 1) * lq)
            s = suf_s[rows, 0:tk]
            carry = jnp.zeros((lq, 1), F32) if first else carry_ref[:, cols][:, 0:1]
            a = jnp.exp(lb_s[rows, 0:tk] + (s - lk_s[rows, 0:tk]) + carry)
            if mask is not None:
                a = jnp.where(mask, a, 0.0)
            out = jnp.dot(a.astype(BF16), get_v(h), preferred_element_type=F32)
            if first:
                acc_ref[:, cols] = out
            else:
                acc_ref[:, cols] += out
            carry_ref[:, cols] = jnp.broadcast_to(carry + s[:, 0:1], (lq, HEAD_DIM))

    start(0, 0)
    heads_tile(lambda h: kn_ref[:, h * HEAD_DIM:(h + 1) * HEAD_DIM].astype(BF16),
               lambda h: vn_ref[:, h * HEAD_DIM:(h + 1) * HEAD_DIM].astype(BF16),
               lq, _strict_lower(lq, lq), True)

    def cond(st):
        return jnp.logical_and(st[0] < nkt, st[1] > 0)

    def body(st):
        t = st[0]
        slot = t % 2
        wait(t, slot)

        @pl.when(t + 1 < nkt)
        def _():
            start(t + 1, 1 - slot)

        rows = pl.ds(pl.multiple_of(slot * tkc, tkc), tkc)
        heads_tile(lambda h: kbuf[rows, h, :].astype(BF16), lambda h: vbuf[rows, h, :].astype(BF16),
                   tkc, None, False)
        return t + 1, _alive(carry_ref[...])

    t_end, _ = lax.while_loop(cond, body, (jnp.int32(0), _alive(carry_ref[...])))

    @pl.when(t_end < nkt)
    def _():
        wait(t_end, t_end % 2)

    o_ref[...] = acc_ref[...].astype(o_ref.dtype)


def _attn_sample(q, k, v, cache_k, cache_v, row_blk0, lq):
    nb, past, nh, hd = cache_k.shape
    d = nh * hd
    tkc = min(SAMPLE_TKC, past)
    assert past % tkc == 0
    nkt = past // tkc
    nrow = nh * lq
    qspec = pl.BlockSpec((lq, d), lambda b: (row_blk0 + b, 0))
    new = pl.BlockSpec((lq, d), lambda b: (b, 0))
    hbm = pl.BlockSpec(memory_space=pl.ANY)
    return pl.pallas_call(
        functools.partial(_attn_sample_kernel, nh=nh, lq=lq, tkc=tkc, nkt=nkt),
        out_shape=jax.ShapeDtypeStruct((nb * lq, d), BF16),
        grid=(nb,),
        in_specs=[qspec, new, new, hbm, hbm],
        out_specs=pl.BlockSpec((lq, d), lambda b: (b, 0)),
        scratch_shapes=[
            pltpu.VMEM((2 * tkc, nh, hd), F32), pltpu.VMEM((2 * tkc, nh, hd), F32),
            pltpu.VMEM((nrow, tkc), F32), pltpu.VMEM((nrow, tkc), F32),
            pltpu.VMEM((2 * nrow, tkc), BF16), pltpu.VMEM((nrow, tkc), F32),
            pltpu.VMEM((lq, d), F32), pltpu.VMEM((lq, d), F32),
            pltpu.SemaphoreType.DMA((2, 2)),
        ],
        compiler_params=_cparams(("arbitrary",)),
        name="attn_sample",
    )(q, k, v, cache_k, cache_v)


def kernel(x_prompt, x_sample, c_prompt, c_sample, state_conv, state_pool, cache_k, cache_v, norm_g, w_mod, b_mod, w_in_ab, b_in_ab, conv_w, conv_b, ln_g, ln_b, pool_w, pool_b, pool_scale, w_out_ab, w_qkv, w_o, w_router, b_router, w_up, b_up, w_down, b_down):
    nbp, lp, d = x_prompt.shape
    nbs, ls, _ = x_sample.shape
    past = cache_k.shape[1]
    conv_ch = conv_w.shape[1]
    pool_ch = pool_scale.shape[0]
    nh = d // HEAD_DIM
    tp, ts = nbp * lp, nbs * ls
    t = tp + ts
    assert lp % CHUNK == 0 and ls == CHUNK and t % ROW_TILE == 0 and tp % COMBINE_TT == 0
    assert ls >= CONV_BUF and lp >= CONV_BUF and conv_ch == pool_ch and d % conv_ch == 0

    cpp = lp // CHUNK
    seq_np = np.concatenate([np.repeat(np.arange(nbp), cpp), nbp + np.arange(nbs)]).astype(np.int32)
    start_np = np.concatenate([np.tile(np.arange(cpp) == 0, nbp), np.ones(nbs, bool)]).astype(np.int32)
    pos_np = np.concatenate([np.tile(np.arange(cpp) * CHUNK, nbp), np.full(nbs, past)]).astype(np.int32)
    seq_of_chunk, chunk_start, chunk_pos = jnp.asarray(seq_np), jnp.asarray(start_np), jnp.asarray(pos_np)

    nseq = nbp + nbs
    nseq_pad = -(-nseq // 16) * 16
    c_all = jnp.zeros((nseq_pad, d), F32).at[:nbp].set(c_prompt).at[nbp:nseq].set(c_sample)
    mod = _modulation(c_all, w_mod, b_mod)
    modc_all = mod[:, seq_of_chunk, :][:, :, None, :]

    x = jnp.concatenate([x_prompt.reshape(tp, d), x_sample.reshape(ts, d)], axis=0)

    conv_state = jnp.zeros((nseq, CONV_PAD, conv_ch), F32).at[nbp:, CONV_PAD - CONV_BUF:].set(state_conv)
    pool_state = jnp.zeros((nseq, POOL_PAD, pool_ch), F32).at[nbp:, POOL_PAD - POOL_BUF:].set(state_pool)

    depth = norm_g.shape[0]
    conv_new = pool_new = k_p = k_s = v_p = v_s = None
    for layer in range(depth):
        modc = modc_all[layer]
        h = _norm_mod(x, norm_g[layer, 0], modc, sc_col=1, sh_col=0)
        if layer % 2 == 0:
            u = _matmul(h, w_in_ab, b_in_ab)
            a_out, conv_new = _conv_branch(u, conv_state, conv_w, conv_b, ln_g, ln_b,
                                           seq_of_chunk, chunk_start)
            b_out, pool_new = _pool_branch(u, (2 * conv_ch) // pool_ch, pool_state, pool_w, pool_b,
                                           pool_scale, seq_of_chunk, chunk_start, chunk_pos)
            y = _matmul(jnp.concatenate([a_out, b_out], axis=1), w_out_ab, None)
        else:
            q = _matmul(h, w_qkv, None, col0=0, ncols=d)
            k_p = _matmul(h, w_qkv, None, col0=d, ncols=d, row0=0, nrows=tp)
            k_s = _matmul(h, w_qkv, None, col0=d, ncols=d, row0=tp, nrows=ts)
            v_p = _matmul(h, w_qkv, None, col0=2 * d, ncols=d, row0=0, nrows=tp)
            v_s = _matmul(h, w_qkv, None, col0=2 * d, ncols=d, row0=tp, nrows=ts)
            o_p = _attn_prompt(q, k_p, v_p, nbp, lp)
            o_s = _attn_sample(q, k_s, v_s, cache_k, cache_v, tp // ls, ls)
            y = _matmul(jnp.concatenate([o_p, o_s], axis=0), w_o, None)
        x1, h3, route, gates_pad, counts = _resid_router(
            x, y, norm_g[layer, 1], norm_g[layer, 2], modc, 2, 4, 3, w_router[layer], b_router[layer])
        counts = counts[0, :w_router.shape[2]].astype(jnp.int32)
        x = _moe_layer(x1, h3, route, counts, gates_pad, layer, w_up, b_up, w_down, b_down,
                       norm_g[layer, 3], modc, 5, tp if layer == depth - 1 else None)

    y_prompt = x[0].reshape(nbp, lp, d)
    y_sample = x[1].reshape(nbs, ls, d)
    conv_new = conv_new[:, CONV_PAD - CONV_BUF:]
    pool_new = pool_new[:, POOL_PAD - POOL_BUF:]
    return (y_prompt, y_sample,
            conv_new[:nbp], conv_new[nbp:], pool_new[:nbp], pool_new[nbp:],
            k_p.reshape(nbp, lp, nh, HEAD_DIM), k_s.reshape(nbs, ls, nh, HEAD_DIM),
            v_p.reshape(nbp, lp, nh, HEAD_DIM), v_s.reshape(nbs, ls, nh, HEAD_DIM))
```

```python
import functools
import math

import numpy as np
import jax
import jax.numpy as jnp
from jax import lax
from jax.experimental import pallas as pl
from jax.experimental.pallas import tpu as pltpu

F32 = jnp.float32
BF16 = jnp.bfloat16

LANES = 128
VMEM_LIMIT = 56 * 1024 * 1024

RMS_EPS = 1e-6
LN_EPS = 1e-5
CONV_WIDTH = 31
CONV_BUF = CONV_WIDTH - 1
POOL_WINDOWS = (2, 4, 8, 16)
POOL_BUF = max(POOL_WINDOWS) - 1
HEAD_DIM = 128
TOP_K = 4
SWIGLU_LIMIT = 7.0
SWIGLU_ALPHA = 1.702
SB_SCALE = 1.0 / math.sqrt(HEAD_DIM)
UNDERFLOW = -104.0

CHUNK = 64
ROW_TILE = 256
MOE_TM = 256
COMBINE_TT = 128
CONV_PAD = 32
POOL_PAD = 16
SB_BLOCK = 128
SB_WINDOW_BLOCKS = 3
SB_TILES_PER_STEP = 8
SAMPLE_TKC = 256
BIG_POS = 2 ** 30
SLAB_PAD = 8


def _cparams(sem, vmem=VMEM_LIMIT):
    return pltpu.CompilerParams(dimension_semantics=sem, vmem_limit_bytes=vmem)


def _rms(x):
    return x * lax.rsqrt(jnp.mean(x * x, axis=-1, keepdims=True) + RMS_EPS)


def _mod_kernel(c_ref, w_ref, b_ref, o_ref):
    c = c_ref[...]
    s = (c * jax.nn.sigmoid(c)).astype(BF16)
    o_ref[...] = jnp.dot(s, w_ref[...].astype(BF16), preferred_element_type=F32) + b_ref[...]


def _modulation(c_all, w_mod, b_mod, tn=512):
    depth, d, n = w_mod.shape
    ns = c_all.shape[0]
    return pl.pallas_call(
        _mod_kernel,
        out_shape=jax.ShapeDtypeStruct((depth, ns, n), F32),
        grid=(depth, n // tn),
        in_specs=[
            pl.BlockSpec((ns, d), lambda l, j: (0, 0)),
            pl.BlockSpec((None, d, tn), lambda l, j: (l, 0, j)),
            pl.BlockSpec((None, 1, tn), lambda l, j: (l, 0, j)),
        ],
        out_specs=pl.BlockSpec((None, ns, tn), lambda l, j: (l, 0, j)),
        compiler_params=_cparams(("arbitrary", "arbitrary")),
        name="modulation",
    )(c_all, w_mod, b_mod.reshape(depth, 1, n))


def _norm_mod_kernel(x_ref, g_ref, sc_ref, sh_ref, o_ref, *, nch):
    for c in range(nch):
        rows = slice(c * CHUNK, (c + 1) * CHUNK)
        y = _rms(x_ref[rows, :]) * g_ref[...]
        o_ref[rows, :] = (y * (1.0 + sc_ref[c]) + sh_ref[c]).astype(o_ref.dtype)


def _norm_mod(x, g, modc, sc_col, sh_col):
    t, d = x.shape
    nch = ROW_TILE // CHUNK
    mspec = lambda col: pl.BlockSpec((nch, 1, d), lambda i: (i, 0, col))
    return pl.pallas_call(
        functools.partial(_norm_mod_kernel, nch=nch),
        out_shape=jax.ShapeDtypeStruct((t, d), BF16),
        grid=(t // ROW_TILE,),
        in_specs=[
            pl.BlockSpec((ROW_TILE, d), lambda i: (i, 0)),
            pl.BlockSpec((1, d), lambda i: (0, 0)),
            mspec(sc_col), mspec(sh_col),
        ],
        out_specs=pl.BlockSpec((ROW_TILE, d), lambda i: (i, 0)),
        compiler_params=_cparams(("arbitrary",)),
        name="norm_mod",
    )(x, g.reshape(1, d), modc, modc)


def _mm_kernel(a_ref, w_ref, b_ref, o_ref, w_bf):
    @pl.when(pl.program_id(1) == 0)
    def _():
        w_bf[...] = w_ref[...].astype(BF16)

    o_ref[...] = (jnp.dot(a_ref[...], w_bf[...], preferred_element_type=F32)
                  + b_ref[...]).astype(o_ref.dtype)


def _matmul(a, w, bias, *, col0=0, ncols=None, row0=0, nrows=None, out_dtype=F32, tm=1024, tn=512):
    kdim = a.shape[1]
    m = a.shape[0] if nrows is None else nrows
    n_total = w.shape[1]
    ncols = n_total if ncols is None else ncols
    tm = math.gcd(tm, m)
    tn = math.gcd(tn, ncols)
    assert a.dtype == BF16 and tm % 16 == 0 and tn % LANES == 0 and col0 % tn == 0 and row0 % tm == 0
    joff = col0 // tn
    ioff = row0 // tm
    if bias is None:
        bias = jnp.zeros((n_total,), F32)
    return pl.pallas_call(
        _mm_kernel,
        out_shape=jax.ShapeDtypeStruct((m, ncols), out_dtype),
        grid=(ncols // tn, m // tm),
        in_specs=[
            pl.BlockSpec((tm, kdim), lambda j, i: (i + ioff, 0)),
            pl.BlockSpec((kdim, tn), lambda j, i: (0, j + joff)),
            pl.BlockSpec((1, tn), lambda j, i: (0, j + joff)),
        ],
        out_specs=pl.BlockSpec((tm, tn), lambda j, i: (i, j)),
        scratch_shapes=[pltpu.VMEM((kdim, tn), BF16)],
        compiler_params=_cparams(("arbitrary", "arbitrary")),
        name="matmul",
    )(a, w, bias.reshape(1, n_total))


def _conv_kernel(seq_ref, start_ref, av_ref, ag_ref, st_ref, w_ref, cb_ref, lg_ref, lb_ref,
                 o_ref, sto_ref, gbuf, cbuf, *, cblk):
    i = pl.program_id(0)
    ch = av_ref.shape[1]

    @pl.when(start_ref[i] == 1)
    def _():
        gbuf[0:CONV_PAD, :] = st_ref[...]

    @pl.when(start_ref[i] == 0)
    def _():
        gbuf[0:CONV_PAD, :] = gbuf[CHUNK:CHUNK + CONV_PAD, :]

    gbuf[CONV_PAD:CONV_PAD + CHUNK, :] = av_ref[...] * jax.nn.sigmoid(ag_ref[...])

    off = CONV_PAD - CONV_BUF
    for c0 in range(0, ch, cblk):
        cols = slice(c0, c0 + cblk)
        acc = jnp.zeros((CHUNK, cblk), F32)
        for j in range(CONV_WIDTH):
            acc = acc + w_ref[j:j + 1, cols] * gbuf[off + j:off + j + CHUNK, cols]
        cbuf[:, cols] = acc + cb_ref[:, cols]

    conv = cbuf[...]
    mu = jnp.mean(conv, axis=-1, keepdims=True)
    cen = conv - mu
    var = jnp.mean(cen * cen, axis=-1, keepdims=True)
    y = cen * lax.rsqrt(var + LN_EPS) * lg_ref[...] + lb_ref[...]
    o_ref[...] = (y * jax.nn.sigmoid(y)).astype(o_ref.dtype)
    sto_ref[...] = gbuf[CHUNK:CHUNK + CONV_PAD, :]


def _conv_branch(u, state_pad, conv_w, conv_b, ln_g, ln_b, seq_of_chunk, chunk_start):
    t = u.shape[0]
    ch = conv_w.shape[1]
    nseq = state_pad.shape[0]
    nchunks = t // CHUNK
    grid_spec = pltpu.PrefetchScalarGridSpec(
        num_scalar_prefetch=2,
        grid=(nchunks,),
        in_specs=[
            pl.BlockSpec((CHUNK, ch), lambda i, s, f: (i, 0)),
            pl.BlockSpec((CHUNK, ch), lambda i, s, f: (i, 1)),
            pl.BlockSpec((None, CONV_PAD, ch), lambda i, s, f: (s[i], 0, 0)),
            pl.BlockSpec((CONV_WIDTH, ch), lambda i, s, f: (0, 0)),
            pl.BlockSpec((1, ch), lambda i, s, f: (0, 0)),
            pl.BlockSpec((1, ch), lambda i, s, f: (0, 0)),
            pl.BlockSpec((1, ch), lambda i, s, f: (0, 0)),
        ],
        out_specs=[
            pl.BlockSpec((CHUNK, ch), lambda i, s, f: (i, 0)),
            pl.BlockSpec((None, CONV_PAD, ch), lambda i, s, f: (s[i], 0, 0)),
        ],
        scratch_shapes=[pltpu.VMEM((CONV_PAD + CHUNK, ch), F32), pltpu.VMEM((CHUNK, ch), F32)],
    )
    return pl.pallas_call(
        functools.partial(_conv_kernel, cblk=min(256, ch)),
        out_shape=[jax.ShapeDtypeStruct((t, ch), BF16),
                   jax.ShapeDtypeStruct((nseq, CONV_PAD, ch), F32)],
        grid_spec=grid_spec,
        compiler_params=_cparams(("arbitrary",)),
        name="conv_branch",
    )(seq_of_chunk, chunk_start, u, u, state_pad, conv_w, conv_b.reshape(1, ch),
      ln_g.reshape(1, ch), ln_b.reshape(1, ch))


def _pool_kernel(seq_ref, start_ref, pos_ref, p_ref, st_ref, w_ref, b_ref, sc_ref,
                 o_ref, sto_ref, pbuf):
    i = pl.program_id(0)
    gch = w_ref.shape[1]

    @pl.when(start_ref[i] == 1)
    def _():
        pbuf[0:POOL_PAD, :] = st_ref[...]

    @pl.when(start_ref[i] == 0)
    def _():
        pbuf[0:POOL_PAD, :] = pbuf[CHUNK:CHUNK + POOL_PAD, :]

    pbuf[POOL_PAD:POOL_PAD + CHUNK, :] = p_ref[...]

    pos = pos_ref[i] + lax.broadcasted_iota(jnp.int32, (CHUNK, 1), 0)
    for gi, win in enumerate(POOL_WINDOWS):
        cols = slice(gi * gch, (gi + 1) * gch)
        cur = pbuf[POOL_PAD:POOL_PAD + CHUNK, cols]
        wsum = cur
        for j in range(1, win):
            wsum = wsum + pbuf[POOL_PAD - j:POOL_PAD - j + CHUNK, cols]
        count = jnp.minimum(pos + 1, win).astype(F32)
        diff = wsum / count - cur
        y = jnp.dot(diff.astype(BF16), w_ref[gi], preferred_element_type=F32) + b_ref[gi]
        o_ref[:, cols] = (y * sc_ref[:, cols]).astype(o_ref.dtype)
    sto_ref[...] = pbuf[CHUNK:CHUNK + POOL_PAD, :]


def _pool_branch(u, col_block, state_pad, pool_w, pool_b, pool_scale,
                 seq_of_chunk, chunk_start, chunk_pos):
    t = u.shape[0]
    ng, gch, _ = pool_w.shape
    ch = ng * gch
    nseq = state_pad.shape[0]
    grid_spec = pltpu.PrefetchScalarGridSpec(
        num_scalar_prefetch=3,
        grid=(t // CHUNK,),
        in_specs=[
            pl.BlockSpec((CHUNK, ch), lambda i, s, f, p: (i, col_block)),
            pl.BlockSpec((None, POOL_PAD, ch), lambda i, s, f, p: (s[i], 0, 0)),
            pl.BlockSpec((ng, gch, gch), lambda i, s, f, p: (0, 0, 0)),
            pl.BlockSpec((ng, 1, gch), lambda i, s, f, p: (0, 0, 0)),
            pl.BlockSpec((1, ch), lambda i, s, f, p: (0, 0)),
        ],
        out_specs=[
            pl.BlockSpec((CHUNK, ch), lambda i, s, f, p: (i, 0)),
            pl.BlockSpec((None, POOL_PAD, ch), lambda i, s, f, p: (s[i], 0, 0)),
        ],
        scratch_shapes=[pltpu.VMEM((POOL_PAD + CHUNK, ch), F32)],
    )
    return pl.pallas_call(
        _pool_kernel,
        out_shape=[jax.ShapeDtypeStruct((t, ch), BF16),
                   jax.ShapeDtypeStruct((nseq, POOL_PAD, ch), F32)],
        grid_spec=grid_spec,
        compiler_params=_cparams(("arbitrary",)),
        name="pool_branch",
    )(seq_of_chunk, chunk_start, chunk_pos, u, state_pad, pool_w.astype(BF16),
      pool_b.reshape(ng, 1, gch), pool_scale.reshape(1, ch))


def _resid_router_kernel(x_ref, y_ref, g1_ref, g2_ref, gt_ref, sc_ref, sh_ref, wr_ref, br_ref,
                         x1_ref, h3_ref, idx_ref, gate_ref, cnt_ref, h_ref, *, nch):
    nsub = h3_ref.shape[1]
    for c in range(nch):
        rows = slice(c * CHUNK, (c + 1) * CHUNK)
        x1 = x_ref[rows, :] + gt_ref[c] * (_rms(y_ref[rows, :]) * g1_ref[...])
        x1_ref[rows, :] = x1
        h_ref[rows, :] = (_rms(x1) * g2_ref[...]) * (1.0 + sc_ref[c]) + sh_ref[c]

    for kk in range(nsub):
        h3_ref[:, kk, :] = h_ref[:, kk * LANES:(kk + 1) * LANES]

    logits = jnp.dot(h_ref[...].astype(BF16), wr_ref[...], preferred_element_type=F32) + br_ref[...]
    lane = lax.broadcasted_iota(jnp.int32, logits.shape, 1).astype(F32)
    vals, idxs = [], []
    for _ in range(TOP_K):
        m = jnp.max(logits, axis=-1, keepdims=True)
        sel = jnp.min(jnp.where(logits == m, lane, float(LANES)), axis=-1, keepdims=True)
        vals.append(m)
        idxs.append(sel)
        logits = jnp.where(lane == sel, -jnp.inf, logits)
    exps = [jnp.exp(v - vals[0]) for v in vals]
    den = exps[0] + exps[1] + exps[2] + exps[3]

    @pl.when(pl.program_id(0) == 0)
    def _():
        cnt_ref[...] = jnp.zeros_like(cnt_ref)

    tr = logits.shape[0]
    onehot = jnp.zeros(logits.shape, F32)
    for k in range(TOP_K):
        onehot = onehot + jnp.where(lane == idxs[k], 1.0, 0.0)
    rr = lax.broadcasted_iota(jnp.int32, (tr, tr), 0)
    cc = lax.broadcasted_iota(jnp.int32, (tr, tr), 1)
    lower = jnp.where(cc < rr, 1.0, 0.0).astype(BF16)
    before = jnp.dot(lower, onehot.astype(BF16), preferred_element_type=F32) + cnt_ref[...]
    cnt_ref[...] += jnp.sum(onehot, axis=0, keepdims=True)

    gate_out = jnp.zeros(logits.shape, F32)
    idx_out = jnp.zeros(logits.shape, F32)
    for k in range(TOP_K):
        rank = jnp.sum(jnp.where(lane == idxs[k], before, 0.0), axis=-1, keepdims=True)
        gate_out = jnp.where(lane == float(k), exps[k] / den, gate_out)
        idx_out = jnp.where(lane == float(k), idxs[k], idx_out)
        idx_out = jnp.where(lane == float(TOP_K + k), rank, idx_out)
    gate_ref[...] = gate_out
    idx_ref[...] = idx_out.astype(jnp.int32)


def _resid_router(x, y, g1, g2, modc, gt_col, sc_col, sh_col, w_router, b_router):
    t, d = x.shape
    ne = w_router.shape[1]
    nch = ROW_TILE // CHUNK
    wr = jnp.zeros((d, LANES), BF16).at[:, :ne].set(w_router.astype(BF16))
    br = jnp.full((1, LANES), -1e30, F32).at[0, :ne].set(b_router.astype(F32))
    mspec = lambda col: pl.BlockSpec((nch, 1, d), lambda i: (i, 0, col))
    row = pl.BlockSpec((ROW_TILE, d), lambda i: (i, 0))
    vec = pl.BlockSpec((1, d), lambda i: (0, 0))
    small = pl.BlockSpec((ROW_TILE, LANES), lambda i: (i, 0))
    return pl.pallas_call(
        functools.partial(_resid_router_kernel, nch=nch),
        out_shape=[jax.ShapeDtypeStruct((t, d), F32),
                   jax.ShapeDtypeStruct((t, d // LANES, LANES), F32),
                   jax.ShapeDtypeStruct((t, LANES), jnp.int32),
                   jax.ShapeDtypeStruct((t, LANES), F32),
                   jax.ShapeDtypeStruct((1, LANES), F32)],
        grid=(t // ROW_TILE,),
        in_specs=[row, row, vec, vec, mspec(gt_col), mspec(sc_col), mspec(sh_col),
                  pl.BlockSpec((d, LANES), lambda i: (0, 0)),
                  pl.BlockSpec((1, LANES), lambda i: (0, 0))],
        out_specs=[row, pl.BlockSpec((ROW_TILE, d // LANES, LANES), lambda i: (i, 0, 0)),
                   small, small, pl.BlockSpec((1, LANES), lambda i: (0, 0))],
        scratch_shapes=[pltpu.VMEM((ROW_TILE, d), F32)],
        compiler_params=_cparams(("arbitrary",)),
        name="resid_router",
    )(x, y, g1.reshape(1, d), g2.reshape(1, d), modc, modc, modc, wr, br)


def _moe_gather_kernel(tok_ref, nv_ref, h3_hbm, o_ref, buf, tmp, sem, *, tm, nsub):
    i = pl.program_id(0)
    nv = nv_ref[0]

    def issue(tile, slot):
        def body(r, carry):
            tok = tok_ref[tile * tm + r]
            pltpu.make_async_copy(h3_hbm.at[tok], buf.at[slot * tm + r, pl.ds(0, nsub)],
                                  sem.at[slot]).start()
            return carry
        lax.fori_loop(0, tm, body, 0, unroll=8)

    @pl.when(i == 0)
    def _():
        issue(0, 0)

    @pl.when(i + 1 < nv)
    def _():
        issue(i + 1, (i + 1) % 2)

    @pl.when(i < nv)
    def _():
        slot = i % 2
        base = pl.multiple_of(slot * tm, tm)
        pltpu.make_async_copy(h3_hbm.at[pl.ds(0, tm)], buf.at[pl.ds(base, tm), pl.ds(0, nsub)],
                              sem.at[slot]).wait()
        for kk in range(nsub):
            tmp[:, kk * LANES:(kk + 1) * LANES] = buf[pl.ds(base, tm), kk, :]
        o_ref[...] = tmp[...].astype(o_ref.dtype)

    @pl.when(i >= nv)
    def _():
        o_ref[...] = jnp.zeros_like(o_ref)


def _moe_gather(slot_token, n_valid, h3, n_tiles):
    t, nsub, _ = h3.shape
    tm = MOE_TM
    grid_spec = pltpu.PrefetchScalarGridSpec(
        num_scalar_prefetch=2,
        grid=(n_tiles,),
        in_specs=[pl.BlockSpec(memory_space=pl.ANY)],
        out_specs=pl.BlockSpec((tm, nsub * LANES), lambda i, tok, nv: (i, 0)),
        scratch_shapes=[pltpu.VMEM((2 * tm, nsub + SLAB_PAD, LANES), F32),
                        pltpu.VMEM((tm, nsub * LANES), F32), pltpu.SemaphoreType.DMA((2,))],
    )
    return pl.pallas_call(
        functools.partial(_moe_gather_kernel, tm=tm, nsub=nsub),
        out_shape=jax.ShapeDtypeStruct((n_tiles * tm, nsub * LANES), BF16),
        grid_spec=grid_spec,
        compiler_params=_cparams(("arbitrary",)),
        name="moe_gather",
    )(slot_token, n_valid, h3)


def _cast_rows(src, dst, rows=256):
    n = src.shape[0]
    for r in range(0, n, rows):
        dst[r:r + rows, :] = src[r:r + rows, :].astype(dst.dtype)


def _moe_up_kernel(te_ref, first_ref, nxt_ref, rows_ref, x_ref, w_hbm, bg_ref, bl_ref, o_ref,
                   stage, w_bf, sem, *, layer, tn, nj):
    j = pl.program_id(0)
    i = pl.program_id(1)

    def copies(e, jj):
        glu = pl.ds(pl.multiple_of(jj * tn, tn), tn)
        lin = pl.ds(pl.multiple_of((nj + jj) * tn, tn), tn)
        return (pltpu.make_async_copy(w_hbm.at[layer, e, :, glu], stage.at[0], sem.at[0]),
                pltpu.make_async_copy(w_hbm.at[layer, e, :, lin], stage.at[1], sem.at[1]))

    def start(e, jj):
        for c in copies(e, jj):
            c.start()

    @pl.when(jnp.logical_and(i == 0, j == 0))
    def _():
        start(te_ref[0], 0)

    @pl.when(first_ref[i] == 1)
    def _():
        for c in copies(te_ref[i], j):
            c.wait()
        _cast_rows(stage.at[0], w_bf.at[0])
        _cast_rows(stage.at[1], w_bf.at[1])
        nxt = nxt_ref[i]

        @pl.when(nxt >= 0)
        def _():
            start(nxt, j)

        @pl.when(jnp.logical_and(nxt < 0, j + 1 < nj))
        def _():
            start(te_ref[0], j + 1)

    def compute(rows):
        x = x_ref[rows, :]
        hg = jnp.dot(x, w_bf[0], preferred_element_type=F32) + bg_ref[...]
        hl = jnp.dot(x, w_bf[1], preferred_element_type=F32) + bl_ref[...]
        x_glu = jnp.minimum(hg, SWIGLU_LIMIT)
        x_lin = jnp.clip(hl, -SWIGLU_LIMIT, SWIGLU_LIMIT)
        act = x_glu * jax.nn.sigmoid(SWIGLU_ALPHA * x_glu) * (x_lin + 1.0)
        o_ref[rows, :] = act.astype(o_ref.dtype)

    tm = x_ref.shape[0]
    nrows = rows_ref[i]

    @pl.when(nrows > tm // 2)
    def _():
        compute(slice(0, tm))

    @pl.when(nrows <= tm // 2)
    def _():
        o_ref[tm // 2:tm, :] = jnp.zeros((tm - tm // 2, o_ref.shape[1]), o_ref.dtype)

    @pl.when(jnp.logical_and(nrows > 0, nrows <= tm // 2))
    def _():
        compute(slice(0, tm // 2))

    @pl.when(nrows == 0)
    def _():
        o_ref[0:tm // 2, :] = jnp.zeros((tm // 2, o_ref.shape[1]), o_ref.dtype)


def _moe_up(x_sorted, w_up, b_up, layer, tile_expert, tile_first, tile_next, tile_rows, tn=512):
    n_pad, d = x_sorted.shape
    _, ne, _, f2 = w_up.shape
    f = f2 // 2
    tm = MOE_TM
    n_tiles = n_pad // tm
    tn = min(tn, f)
    nj = f // tn
    grid_spec = pltpu.PrefetchScalarGridSpec(
        num_scalar_prefetch=4,
        grid=(nj, n_tiles),
        in_specs=[
            pl.BlockSpec((tm, d), lambda j, i, te, fi, nx, nv: (i, 0)),
            pl.BlockSpec(memory_space=pl.ANY),
            pl.BlockSpec((None, None, 1, tn), lambda j, i, te, fi, nx, nv: (layer, te[i], 0, j)),
            pl.BlockSpec((None, None, 1, tn), lambda j, i, te, fi, nx, nv: (layer, te[i], 0, nj + j)),
        ],
        out_specs=pl.BlockSpec((tm, tn), lambda j, i, te, fi, nx, nv: (i, j)),
        scratch_shapes=[pltpu.VMEM((2, d, tn), F32), pltpu.VMEM((2, d, tn), BF16),
                        pltpu.SemaphoreType.DMA((2,))],
    )
    b4 = b_up.reshape(b_up.shape[0], ne, 1, f2)
    return pl.pallas_call(
        functools.partial(_moe_up_kernel, layer=layer, tn=tn, nj=nj),
        out_shape=jax.ShapeDtypeStruct((n_pad, f), BF16),
        grid_spec=grid_spec,
        compiler_params=_cparams(("arbitrary", "arbitrary")),
        name="moe_up",
    )(tile_expert, tile_first, tile_next, tile_rows, x_sorted, w_up, b4, b4)


def _moe_down_kernel(te_ref, first_ref, nxt_ref, rows_ref, a_ref, w_hbm, b_ref, o_ref,
                     stage, w_bf, sem, *, layer, tn, nj):
    j = pl.program_id(0)
    i = pl.program_id(1)

    def copy(e, jj):
        cols = pl.ds(pl.multiple_of(jj * tn, tn), tn)
        return pltpu.make_async_copy(w_hbm.at[layer, e, :, cols], stage, sem.at[0])

    @pl.when(jnp.logical_and(i == 0, j == 0))
    def _():
        copy(te_ref[0], 0).start()

    @pl.when(first_ref[i] == 1)
    def _():
        copy(te_ref[i], j).wait()
        _cast_rows(stage, w_bf)
        nxt = nxt_ref[i]

        @pl.when(nxt >= 0)
        def _():
            copy(nxt, j).start()

        @pl.when(jnp.logical_and(nxt < 0, j + 1 < nj))
        def _():
            copy(te_ref[0], j + 1).start()

    nsub = o_ref.shape[1]

    def compute(rows):
        res = jnp.dot(a_ref[rows, :], w_bf[...], preferred_element_type=F32) + b_ref[...]
        for c in range(nsub):
            o_ref[rows, c, :] = res[:, c * LANES:(c + 1) * LANES]

    tm = a_ref.shape[0]
    nrows = rows_ref[i]

    @pl.when(nrows > tm // 2)
    def _():
        compute(slice(0, tm))

    @pl.when(nrows <= tm // 2)
    def _():
        o_ref[tm // 2:tm] = jnp.zeros((tm - tm // 2, nsub, LANES), o_ref.dtype)

    @pl.when(jnp.logical_and(nrows > 0, nrows <= tm // 2))
    def _():
        compute(slice(0, tm // 2))

    @pl.when(nrows == 0)
    def _():
        o_ref[0:tm // 2] = jnp.zeros((tm // 2, nsub, LANES), o_ref.dtype)


def _moe_down(act, w_down, b_down, layer, tile_expert, tile_first, tile_next, tile_rows, tn=2048):
    n_pad, f = act.shape
    _, ne, _, d = w_down.shape
    tm = MOE_TM
    n_tiles = n_pad // tm
    tn = min(tn, d)
    nj = d // tn
    nsub = tn // LANES
    grid_spec = pltpu.PrefetchScalarGridSpec(
        num_scalar_prefetch=4,
        grid=(nj, n_tiles),
        in_specs=[
            pl.BlockSpec((tm, f), lambda j, i, te, fi, nx, nv: (i, 0)),
            pl.BlockSpec(memory_space=pl.ANY),
            pl.BlockSpec((None, None, 1, tn), lambda j, i, te, fi, nx, nv: (layer, te[i], 0, j)),
        ],
        out_specs=pl.BlockSpec((tm, nsub, LANES), lambda j, i, te, fi, nx, nv: (i, j, 0)),
        scratch_shapes=[pltpu.VMEM((f, tn), F32), pltpu.VMEM((f, tn), BF16),
                        pltpu.SemaphoreType.DMA((1,))],
    )
    return pl.pallas_call(
        functools.partial(_moe_down_kernel, layer=layer, tn=tn, nj=nj),
        out_shape=jax.ShapeDtypeStruct((n_pad, d // LANES, LANES), F32),
        grid_spec=grid_spec,
        compiler_params=_cparams(("arbitrary", "arbitrary")),
        name="moe_down",
    )(tile_expert, tile_first, tile_next, tile_rows, act, w_down,
      b_down.reshape(b_down.shape[0], ne, 1, d))


def _moe_combine_kernel(slot_ref, y3_hbm, x_ref, gate_ref, g_ref, gt_ref, *refs,
                        tt, nt, nsub, npt):
    out_refs, (buf, sum3, ybuf, sem) = refs[:-4], refs[-4:]
    i = pl.program_id(0)
    rows_per_slot = TOP_K * tt

    def issue(tile, s):
        def body(r, carry):
            for k in range(TOP_K):
                src = slot_ref[(tile * tt + r) * TOP_K + k]
                pltpu.make_async_copy(y3_hbm.at[src],
                                      buf.at[s * rows_per_slot + k * tt + r, pl.ds(0, nsub)],
                                      sem.at[s]).start()
            return carry
        lax.fori_loop(0, tt, body, 0, unroll=4)

    @pl.when(i == 0)
    def _():
        issue(0, 0)

    @pl.when(i + 1 < nt)
    def _():
        issue(i + 1, (i + 1) % 2)

    s = i % 2
    base = pl.multiple_of(s * rows_per_slot, rows_per_slot)
    pltpu.make_async_copy(y3_hbm.at[pl.ds(0, rows_per_slot)],
                          buf.at[pl.ds(base, rows_per_slot), pl.ds(0, nsub)], sem.at[s]).wait()

    def token_body(r, carry):
        acc = gate_ref[r, 0:1, :] * buf[base + r, pl.ds(0, nsub), :]
        for k in range(1, TOP_K):
            acc = acc + gate_ref[r, k:k + 1, :] * buf[base + k * tt + r, pl.ds(0, nsub), :]
        sum3[r, pl.ds(0, nsub), :] = acc
        return carry
    lax.fori_loop(0, tt, token_body, 0, unroll=4)

    for kk in range(nsub):
        ybuf[:, kk * LANES:(kk + 1) * LANES] = sum3[:, kk, :]
    def write(o_ref):
        for c in range(tt // CHUNK):
            rows = slice(c * CHUNK, (c + 1) * CHUNK)
            o_ref[rows, :] = x_ref[rows, :] + gt_ref[c] * (_rms(ybuf[rows, :]) * g_ref[...])

    if npt is None:
        write(out_refs[0])
    else:
        pl.when(i < npt)(lambda: write(out_refs[0]))
        pl.when(i >= npt)(lambda: write(out_refs[1]))


def _moe_combine(slot, y3, x1, gates_pad, g, modc, gt_col, split=None):
    t, d = x1.shape
    nsub = d // LANES
    tt = COMBINE_TT
    nt = t // tt
    nch = tt // CHUNK
    gates_rep = jnp.broadcast_to(gates_pad[:, :TOP_K, None], (t, TOP_K, LANES))
    if split is None:
        npt = None
        out_shape = jax.ShapeDtypeStruct((t, d), F32)
        out_specs = pl.BlockSpec((tt, d), lambda i, sl: (i, 0))
    else:
        assert split % tt == 0 and 0 < split < t
        npt = split // tt
        out_shape = [jax.ShapeDtypeStruct((split, d), F32), jax.ShapeDtypeStruct((t - split, d), F32)]
        out_specs = [pl.BlockSpec((tt, d), lambda i, sl: (jnp.minimum(i, npt - 1), 0)),
                     pl.BlockSpec((tt, d), lambda i, sl: (jnp.maximum(i - npt, 0), 0))]
    grid_spec = pltpu.PrefetchScalarGridSpec(
        num_scalar_prefetch=1,
        grid=(nt,),
        in_specs=[
            pl.BlockSpec(memory_space=pl.ANY),
            pl.BlockSpec((tt, d), lambda i, sl: (i, 0)),
            pl.BlockSpec((tt, TOP_K, LANES), lambda i, sl: (i, 0, 0)),
            pl.BlockSpec((1, d), lambda i, sl: (0, 0)),
            pl.BlockSpec((nch, 1, d), lambda i, sl: (i, 0, gt_col)),
        ],
        out_specs=out_specs,
        scratch_shapes=[pltpu.VMEM((2 * TOP_K * tt, nsub + SLAB_PAD, LANES), F32),
                        pltpu.VMEM((tt, nsub + SLAB_PAD, LANES), F32),
                        pltpu.VMEM((tt, d), F32), pltpu.SemaphoreType.DMA((2,))],
    )
    return pl.pallas_call(
        functools.partial(_moe_combine_kernel, tt=tt, nt=nt, nsub=nsub, npt=npt),
        out_shape=out_shape,
        grid_spec=grid_spec,
        compiler_params=_cparams(("arbitrary",)),
        name="moe_combine",
    )(slot, y3, x1, gates_rep, g.reshape(1, d), modc)


def _moe_layer(x1, h3, route, counts, gates_pad, layer, w_up, b_up, w_down, b_down, g_post, modc,
               gt_col, split):
    t = h3.shape[0]
    ne = w_up.shape[1]
    tm = MOE_TM
    n_assign = t * TOP_K
    n_tiles = n_assign // tm + ne

    flat_e = route[:, :TOP_K].reshape(-1)
    rank = route[:, TOP_K:2 * TOP_K].reshape(-1)
    padded = (counts + tm - 1) // tm * tm
    pad_end = jnp.cumsum(padded)
    pad_start = pad_end - padded
    slot = (pad_start[flat_e] + rank).astype(jnp.int32)
    slot_token = jnp.zeros((n_tiles * tm,), jnp.int32).at[slot].set(
        jnp.arange(n_assign, dtype=jnp.int32) // TOP_K)
    n_valid = (pad_end[-1] // tm).astype(jnp.int32)
    tile_ids = jnp.arange(n_tiles, dtype=jnp.int32)
    tile_start = jnp.minimum(tile_ids, n_valid - 1) * tm
    tile_expert = jnp.sum((pad_end[None, :] <= tile_start[:, None]).astype(jnp.int32), axis=1)
    tile_expert = jnp.minimum(tile_expert, ne - 1)
    tile_rows = jnp.clip(counts[tile_expert] - (tile_start - pad_start[tile_expert]), 0, tm)
    tile_rows = jnp.where(tile_ids < n_valid, tile_rows, 0).astype(jnp.int32)
    tile_first = jnp.concatenate([jnp.ones((1,), jnp.int32),
                                  (tile_expert[1:] != tile_expert[:-1]).astype(jnp.int32)])
    first_pos = jnp.where(tile_first == 1, tile_ids, n_tiles)
    next_pos = jnp.concatenate([lax.cummin(first_pos, reverse=True)[1:],
                                jnp.full((1,), n_tiles, jnp.int32)])
    tile_next = jnp.where(next_pos < n_tiles, tile_expert[jnp.minimum(next_pos, n_tiles - 1)], -1)
    tile_next = tile_next.astype(jnp.int32)

    nv = n_valid.reshape(1)
    x_sorted = _moe_gather(slot_token, nv, h3, n_tiles)
    act = _moe_up(x_sorted, w_up, b_up, layer, tile_expert, tile_first, tile_next, tile_rows)
    y3 = _moe_down(act, w_down, b_down, layer, tile_expert, tile_first, tile_next, tile_rows)
    return _moe_combine(slot, y3, x1, gates_pad, g_post, modc, gt_col, split)


def _log_gates(z):
    sp = jnp.log(1.0 + jnp.exp(-jnp.abs(z)))
    return jnp.minimum(z, 0.0) - sp, jnp.minimum(-z, 0.0) - sp


def _split_bf16(x):
    hi = x.astype(BF16)
    return hi, (x - hi.astype(F32)).astype(BF16)


def _sb_window(q, k_ref, v_ref, k0, nblk, carry, qpos, klim, tri):
    gates = _sb_scores(q, k_ref, k0, nblk, qpos, klim)
    return _sb_weights(gates, _sb_suffix(gates, tri), carry, v_ref, k0, nblk)


def _sb_scores(q, k_ref, k0, nblk, qpos, klim):
    tb = SB_BLOCK
    tq = q.shape[0]
    col = lax.broadcasted_iota(jnp.int32, (tq, tb), 1)
    lim = jnp.minimum(qpos, klim)
    log_beta, log_keep, masks = [], [], []
    for b in range(nblk):
        kt = k_ref[pl.ds(pl.multiple_of(k0 + b * tb, tb), tb), :]
        z = lax.dot_general(q, kt, (((1,), (1,)), ((), ())), preferred_element_type=F32) * SB_SCALE
        lb, lk = _log_gates(z)
        mask = (k0 + b * tb + col) < lim
        log_beta.append(lb)
        log_keep.append(jnp.where(mask, lk, 0.0))
        masks.append(mask)
    return log_beta, log_keep, masks


def _sb_suffix(gates, tri):
    hi, lo = _split_bf16(jnp.concatenate(gates[1], axis=0))
    return jnp.dot(jnp.concatenate([hi, lo], axis=0), tri, preferred_element_type=F32)


def _sb_weights(gates, suf, carry, v_ref, k0, nblk):
    log_beta, log_keep, masks = gates
    tb = SB_BLOCK
    tq = log_beta[0].shape[0]
    a_blocks = [None] * nblk
    for b in reversed(range(nblk)):
        s = suf[b * tq:(b + 1) * tq] + suf[(nblk + b) * tq:(nblk + b + 1) * tq]
        a = jnp.exp(log_beta[b] + (s - log_keep[b]) + carry)
        a_blocks[b] = jnp.where(masks[b], a, 0.0).astype(BF16)
        carry = carry + s[:, 0:1]
    vw = v_ref[pl.ds(pl.multiple_of(k0, tb), nblk * tb), :]
    out = jnp.dot(jnp.concatenate(a_blocks, axis=1), vw, preferred_element_type=F32)
    return out, carry


def _alive(carry):
    return (jnp.max(carry) > UNDERFLOW).astype(jnp.int32)


def _tri(tk):
    r = lax.broadcasted_iota(jnp.int32, (tk, tk), 0)
    c = lax.broadcasted_iota(jnp.int32, (tk, tk), 1)
    return jnp.where(r >= c, 1.0, 0.0).astype(BF16)


def _strict_lower(tq, tk):
    r = lax.broadcasted_iota(jnp.int32, (tq, tk), 0)
    c = lax.broadcasted_iota(jnp.int32, (tq, tk), 1)
    return c < r


def _attn_prompt_kernel(q_ref, k_ref, v_ref, o_ref, kb_ref, vb_ref, *, seq, nblk, ntile):
    tb = SB_BLOCK
    kb_ref[...] = k_ref[...].astype(BF16)
    vb_ref[...] = v_ref[...].astype(BF16)
    tri = _tri(tb)
    row = lax.broadcasted_iota(jnp.int32, (tb, 1), 0)

    def tile_setup(qi):
        r0 = pl.multiple_of(qi * tb, tb)
        q = q_ref[pl.ds(r0, tb), :].astype(BF16)
        return r0, q, r0 + row, jnp.maximum(qi - (nblk - 1), 0)

    def finish(r0, q, qpos, ws, acc, carry):
        def cond(st):
            return jnp.logical_and(st[0] > 0, st[1] > 0)

        def body(st):
            ws, _, acc, carry = st
            ws2 = jnp.maximum(ws - nblk, 0)
            out, carry = _sb_window(q, kb_ref, vb_ref, ws2 * tb, nblk, carry, qpos, ws * tb, tri)
            return ws2, _alive(carry), acc + out, carry

        _, _, acc, _ = lax.while_loop(cond, body, (ws, _alive(carry), acc, carry))
        o_ref[pl.ds(r0, tb), :] = acc.astype(o_ref.dtype)

    def group_body(p, unused):
        tiles = [tile_setup(ntile * p + u) for u in range(ntile)]
        gates = [_sb_scores(q, kb_ref, ws * tb, nblk, qpos, BIG_POS) for _, q, qpos, ws in tiles]
        sufs = [_sb_suffix(g, tri) for g in gates]
        zero = jnp.zeros((tb, 1), F32)
        outs = [_sb_weights(g, s, zero, vb_ref, tl[3] * tb, nblk)
                for g, s, tl in zip(gates, sufs, tiles)]
        for (r0, q, qpos, ws), (acc, carry) in zip(tiles, outs):
            finish(r0, q, qpos, ws, acc, carry)
        return unused

    lax.fori_loop(0, seq // (ntile * tb), group_body, 0)


def _attn_prompt(q, k, v, nb, seq):
    d = q.shape[1]
    nh = d // HEAD_DIM
    nblk = min(SB_WINDOW_BLOCKS, seq // SB_BLOCK)
    ntile = math.gcd(SB_TILES_PER_STEP, seq // SB_BLOCK)
    blk = pl.BlockSpec((seq, HEAD_DIM), lambda b, h: (b, h))
    return pl.pallas_call(
        functools.partial(_attn_prompt_kernel, seq=seq, nblk=nblk, ntile=ntile),
        out_shape=jax.ShapeDtypeStruct((nb * seq, d), BF16),
        grid=(nb, nh),
        in_specs=[blk, blk, blk],
        out_specs=blk,
        scratch_shapes=[pltpu.VMEM((seq, HEAD_DIM), BF16), pltpu.VMEM((seq, HEAD_DIM), BF16)],
        compiler_params=_cparams(("arbitrary", "arbitrary")),
        name="attn_prompt",
    )(q, k, v)


def _attn_sample_kernel(q_ref, kn_ref, vn_ref, ck_hbm, cv_hbm, o_ref,
                        kbuf, vbuf, lb_s, lk_s, hl_s, suf_s, acc_ref, carry_ref, sem,
                        *, nh, lq, tkc, nkt):
    b = pl.program_id(0)
    nrow = nh * lq

    def copies(t, slot):
        k0 = pl.multiple_of((nkt - 1 - t) * tkc, tkc)
        dst = pl.ds(pl.multiple_of(slot * tkc, tkc), tkc)
        return (pltpu.make_async_copy(ck_hbm.at[b, pl.ds(k0, tkc)], kbuf.at[dst], sem.at[0, slot]),
                pltpu.make_async_copy(cv_hbm.at[b, pl.ds(k0, tkc)], vbuf.at[dst], sem.at[1, slot]))

    def start(t, slot):
        for c in copies(t, slot):
            c.start()

    def wait(t, slot):
        for c in copies(t, slot):
            c.wait()

    def heads_tile(get_k, get_v, tk, mask, first):
        tri = _tri(tk)
        for h in range(nh):
            cols = slice(h * HEAD_DIM, (h + 1) * HEAD_DIM)
            rows = slice(h * lq, (h + 1) * lq)
            z = lax.dot_general(q_ref[:, cols].astype(BF16), get_k(h), (((1,), (1,)), ((), ())),
                                preferred_element_type=F32) * SB_SCALE
            lb, lk = _log_gates(z)
            if mask is not None:
                lk = jnp.where(mask, lk, 0.0)
            hi, lo = _split_bf16(lk)
            lb_s[rows, 0:tk] = lb
            lk_s[rows, 0:tk] = lk
            hl_s[rows, 0:tk] = hi
            hl_s[nrow + h * lq:nrow + (h + 1) * lq, 0:tk] = lo
        suf = jnp.dot(hl_s[:, 0:tk], tri, preferred_element_type=F32)
        suf_s[:, 0:tk] = suf[0:nrow] + suf[nrow:2 * nrow]
        for h in range(nh):
            cols = slice(h * HEAD_DIM, (h + 1) * HEAD_DIM)
            rows = slice(h * lq, (h + 1) * lq)
            s = suf_s[rows, 0:tk]
            carry = jnp.zeros((lq, 1), F32) if first else carry_ref[:, cols][:, 0:1]
            a = jnp.exp(lb_s[rows, 0:tk] + (s - lk_s[rows, 0:tk]) + carry)
            if mask is not None:
                a = jnp.where(mask, a, 0.0)
            out = jnp.dot(a.astype(BF16), get_v(h), preferred_element_type=F32)
            if first:
                acc_ref[:, cols] = out
            else:
                acc_ref[:, cols] += out
            carry_ref[:, cols] = jnp.broadcast_to(carry + s[:, 0:1], (lq, HEAD_DIM))

    start(0, 0)
    heads_tile(lambda h: kn_ref[:, h * HEAD_DIM:(h + 1) * HEAD_DIM].astype(BF16),
               lambda h: vn_ref[:, h * HEAD_DIM:(h + 1) * HEAD_DIM].astype(BF16),
               lq, _strict_lower(lq, lq), True)

    def cond(st):
        return jnp.logical_and(st[0] < nkt, st[1] > 0)

    def body(st):
        t = st[0]
        slot = t % 2
        wait(t, slot)

        @pl.when(t + 1 < nkt)
        def _():
            start(t + 1, 1 - slot)

        rows = pl.ds(pl.multiple_of(slot * tkc, tkc), tkc)
        heads_tile(lambda h: kbuf[rows, h, :].astype(BF16), lambda h: vbuf[rows, h, :].astype(BF16),
                   tkc, None, False)
        return t + 1, _alive(carry_ref[...])

    t_end, _ = lax.while_loop(cond, body, (jnp.int32(0), _alive(carry_ref[...])))

    @pl.when(t_end < nkt)
    def _():
        wait(t_end, t_end % 2)

    o_ref[...] = acc_ref[...].astype(o_ref.dtype)


def _attn_sample(q, k, v, cache_k, cache_v, row_blk0, lq):
    nb, past, nh, hd = cache_k.shape
    d = nh * hd
    tkc = min(SAMPLE_TKC, past)
    assert past % tkc == 0
    nkt = past // tkc
    nrow = nh * lq
    qspec = pl.BlockSpec((lq, d), lambda b: (row_blk0 + b, 0))
    new = pl.BlockSpec((lq, d), lambda b: (b, 0))
    hbm = pl.BlockSpec(memory_space=pl.ANY)
    return pl.pallas_call(
        functools.partial(_attn_sample_kernel, nh=nh, lq=lq, tkc=tkc, nkt=nkt),
        out_shape=jax.ShapeDtypeStruct((nb * lq, d), BF16),
        grid=(nb,),
        in_specs=[qspec, new, new, hbm, hbm],
        out_specs=pl.BlockSpec((lq, d), lambda b: (b, 0)),
        scratch_shapes=[
            pltpu.VMEM((2 * tkc, nh, hd), F32), pltpu.VMEM((2 * tkc, nh, hd), F32),
            pltpu.VMEM((nrow, tkc), F32), pltpu.VMEM((nrow, tkc), F32),
            pltpu.VMEM((2 * nrow, tkc), BF16), pltpu.VMEM((nrow, tkc), F32),
            pltpu.VMEM((lq, d), F32), pltpu.VMEM((lq, d), F32),
            pltpu.SemaphoreType.DMA((2, 2)),
        ],
        compiler_params=_cparams(("arbitrary",)),
        name="attn_sample",
    )(q, k, v, cache_k, cache_v)


def kernel(x_prompt, x_sample, c_prompt, c_sample, state_conv, state_pool, cache_k, cache_v, norm_g, w_mod, b_mod, w_in_ab, b_in_ab, conv_w, conv_b, ln_g, ln_b, pool_w, pool_b, pool_scale, w_out_ab, w_qkv, w_o, w_router, b_router, w_up, b_up, w_down, b_down):
    nbp, lp, d = x_prompt.shape
    nbs, ls, _ = x_sample.shape
    past = cache_k.shape[1]
    conv_ch = conv_w.shape[1]
    pool_ch = pool_scale.shape[0]
    nh = d // HEAD_DIM
    tp, ts = nbp * lp, nbs * ls
    t = tp + ts
    assert lp % CHUNK == 0 and ls == CHUNK and t % ROW_TILE == 0 and tp % COMBINE_TT == 0
    assert ls >= CONV_BUF and lp >= CONV_BUF and conv_ch == pool_ch and d % conv_ch == 0

    cpp = lp // CHUNK
    seq_np = np.concatenate([np.repeat(np.arange(nbp), cpp), nbp + np.arange(nbs)]).astype(np.int32)
    start_np = np.concatenate([np.tile(np.arange(cpp) == 0, nbp), np.ones(nbs, bool)]).astype(np.int32)
    pos_np = np.concatenate([np.tile(np.arange(cpp) * CHUNK, nbp), np.full(nbs, past)]).astype(np.int32)
    seq_of_chunk, chunk_start, chunk_pos = jnp.asarray(seq_np), jnp.asarray(start_np), jnp.asarray(pos_np)

    nseq = nbp + nbs
    nseq_pad = -(-nseq // 16) * 16
    c_all = jnp.zeros((nseq_pad, d), F32).at[:nbp].set(c_prompt).at[nbp:nseq].set(c_sample)
    mod = _modulation(c_all, w_mod, b_mod)
    modc_all = mod[:, seq_of_chunk, :][:, :, None, :]

    x = jnp.concatenate([x_prompt.reshape(tp, d), x_sample.reshape(ts, d)], axis=0)

    conv_state = jnp.zeros((nseq, CONV_PAD, conv_ch), F32).at[nbp:, CONV_PAD - CONV_BUF:].set(state_conv)
    pool_state = jnp.zeros((nseq, POOL_PAD, pool_ch), F32).at[nbp:, POOL_PAD - POOL_BUF:].set(state_pool)

    depth = norm_g.shape[0]
    conv_new = pool_new = k_p = k_s = v_p = v_s = None
    for layer in range(depth):
        modc = modc_all[layer]
        h = _norm_mod(x, norm_g[layer, 0], modc, sc_col=1, sh_col=0)
        if layer % 2 == 0:
            u = _matmul(h, w_in_ab, b_in_ab)
            a_out, conv_new = _conv_branch(u, conv_state, conv_w, conv_b, ln_g, ln_b,
                                           seq_of_chunk, chunk_start)
            b_out, pool_new = _pool_branch(u, (2 * conv_ch) // pool_ch, pool_state, pool_w, pool_b,
                                           pool_scale, seq_of_chunk, chunk_start, chunk_pos)
            y = _matmul(jnp.concatenate([a_out, b_out], axis=1), w_out_ab, None)
        else:
            q = _matmul(h, w_qkv, None, col0=0, ncols=d)
            k_p = _matmul(h, w_qkv, None, col0=d, ncols=d, row0=0, nrows=tp)
            k_s = _matmul(h, w_qkv, None, col0=d, ncols=d, row0=tp, nrows=ts)
            v_p = _matmul(h, w_qkv, None, col0=2 * d, ncols=d, row0=0, nrows=tp)
            v_s = _matmul(h, w_qkv, None, col0=2 * d, ncols=d, row0=tp, nrows=ts)
            o_p = _attn_prompt(q, k_p, v_p, nbp, lp)
            o_s = _attn_sample(q, k_s, v_s, cache_k, cache_v, tp // ls, ls)
            y = _matmul(jnp.concatenate([o_p, o_s], axis=0), w_o, None)
        x1, h3, route, gates_pad, counts = _resid_router(
            x, y, norm_g[layer, 1], norm_g[layer, 2], modc, 2, 4, 3, w_router[layer], b_router[layer])
        counts = counts[0, :w_router.shape[2]].astype(jnp.int32)
        x = _moe_layer(x1, h3, route, counts, gates_pad, layer, w_up, b_up, w_down, b_down,
                       norm_g[layer, 3], modc, 5, tp if layer == depth - 1 else None)

    y_prompt = x[0].reshape(nbp, lp, d)
    y_sample = x[1].reshape(nbs, ls, d)
    conv_new = conv_new[:, CONV_PAD - CONV_BUF:]
    pool_new = pool_new[:, POOL_PAD - POOL_BUF:]
    return (y_prompt, y_sample,
            conv_new[:nbp], conv_new[nbp:], pool_new[:nbp], pool_new[nbp:],
            k_p.reshape(nbp, lp, nh, HEAD_DIM), k_s.reshape(nbs, ls, nh, HEAD_DIM),
            v_p.reshape(nbp, lp, nh, HEAD_DIM), v_s.reshape(nbs, ls, nh, HEAD_DIM))
```
